```python
import math
import jax, jax.numpy as jnp
from jax import lax
import numpy as np

D_MODEL = 1024
BATCH = 8
SEQ = 2048
DEPTH = 4
DEC_BATCH = 128
DEC_SEQ = 4
PAST_LEN = 2048
PAGE_SIZE = 128

N_HEADS = 8
HEAD_DIM = 64
N_QK = 2 * N_HEADS
QK_W = N_QK * HEAD_DIM
ATT_W = N_HEADS * 2 * HEAD_DIM
D_RNN = 1280
N_RNN_BLOCKS = 10
RNN_BW = D_RNN // N_RNN_BLOCKS
CONV_W = 4
LRU_C = 8.0
D_FF = 2816
ROPE_THETA = 10000.0
Q_BLOCK = 128
NORM_EPS = 1e-6
SUBLN_EPS = 1e-5
NEG_INF = -1e30
SPLIT_IDX = [QK_W, 2 * QK_W, 2 * QK_W + ATT_W, 2 * QK_W + ATT_W + D_RNN,
             2 * QK_W + ATT_W + 2 * D_RNN, 2 * QK_W + ATT_W + 2 * D_RNN + D_MODEL]
IN_W = 2 * QK_W + ATT_W + 2 * D_RNN + 2 * D_MODEL

kernel_name = "hybrid_diffattn_rglru_macaron_adaln_step"


def lambda_init(l):
    return 0.8 - 0.6 * math.exp(-0.3 * l)


def rmsnorm(x, w, eps=NORM_EPS):
    xf = x.astype(jnp.float32)
    y = xf * lax.rsqrt(jnp.mean(xf * xf, axis=-1, keepdims=True) + eps)
    return (y * w.astype(jnp.float32)).astype(x.dtype)


def rope(x, pos):
    half = HEAD_DIM // 2
    inv = 1.0 / (ROPE_THETA ** (jnp.arange(half, dtype=jnp.float32) * 2.0 / HEAD_DIM))
    ang = pos.astype(jnp.float32)[:, None] * inv[None, :]
    cos = jnp.cos(ang)[None, :, None, :]
    sin = jnp.sin(ang)[None, :, None, :]
    xf = x.astype(jnp.float32)
    x1, x2 = xf[..., :half], xf[..., half:]
    return jnp.concatenate([x1 * cos - x2 * sin, x2 * cos + x1 * sin], axis=-1).astype(x.dtype)


def swiglu(h, wi, wo):
    g, u = jnp.split(h @ wi, 2, axis=-1)
    return (jax.nn.silu(g) * u) @ wo


def attn_scores(q, k):
    return jnp.einsum('bqhd,bkhd->bhqk', q, k).astype(jnp.float32) * (HEAD_DIM ** -0.5)


def diff_weights(s, lam):
    p = jax.nn.softmax(s, axis=-1)
    b, _, tq, tk = p.shape
    p = p.reshape(b, N_HEADS, 2, tq, tk)
    return p[:, :, 0] - lam * p[:, :, 1]


def causal_conv(x, buf, w, b):
    xp = jnp.concatenate([buf.astype(x.dtype), x], axis=1)
    y = lax.conv_general_dilated(xp, w[:, None, :].astype(x.dtype), (1,), 'VALID',
                                 dimension_numbers=('NWC', 'WIO', 'NWC'),
                                 feature_group_count=D_RNN)
    return y + b, xp[:, -(CONV_W - 1):]


def block_diag(x, w, b):
    xb = x.reshape(x.shape[0], x.shape[1], N_RNN_BLOCKS, RNN_BW)
    return jnp.einsum('btnc,ncd->btnd', xb, w).reshape(x.shape) + b


def rg_lru(xc, h0, w_a, b_a, w_x, b_x, lru_param):
    r = jax.nn.sigmoid(block_diag(xc, w_a, b_a).astype(jnp.float32))
    i = jax.nn.sigmoid(block_diag(xc, w_x, b_x).astype(jnp.float32))
    log_a = LRU_C * r * jax.nn.log_sigmoid(lru_param.astype(jnp.float32))
    a = jnp.exp(log_a)
    bt = jnp.sqrt(-jnp.expm1(2.0 * log_a)) * (i * xc.astype(jnp.float32))
    bt = bt.at[:, 0].add(a[:, 0] * h0.astype(jnp.float32))

    def combine(e1, e2):
        a1, b1 = e1
        a2, b2 = e2
        return a1 * a2, a2 * b1 + b2

    _, h = lax.associative_scan(combine, (a, bt), axis=1)
    return h, h[:, -1]


def layer(l, x, c, pos, conv_buf, h0, attend, P):
    B, T = x.shape[0], x.shape[1]
    ada = (jax.nn.silu(c) @ P['w_ada'][l] + P['b_ada'][l]).reshape(B, 3, 3, 1, D_MODEL)
    sh, sc, gt = ada[:, :, 0], ada[:, :, 1], ada[:, :, 2]
    h = rmsnorm(x, P['norm_w'][l, 0]) * (1.0 + sc[:, 0]) + sh[:, 0]
    x = x + 0.5 * gt[:, 0] * swiglu(h, P['ffn1_wi'][l], P['ffn1_wo'][l])
    h = rmsnorm(x, P['norm_w'][l, 1]) * (1.0 + sc[:, 1]) + sh[:, 1]
    q, k, v, rx, rg, ga, gr = jnp.split(h @ P['w_in'][l], SPLIT_IDX, axis=-1)
    q = rope(q.reshape(B, T, N_QK, HEAD_DIM), pos)
    k = rope(k.reshape(B, T, N_QK, HEAD_DIM), pos)
    v = v.reshape(B, T, N_HEADS, 2 * HEAD_DIM)
    li = lambda_init(l)
    f32 = jnp.float32
    lam = (jnp.exp(jnp.sum(P['lam_q1'][l].astype(f32) * P['lam_k1'][l].astype(f32)))
           - jnp.exp(jnp.sum(P['lam_q2'][l].astype(f32) * P['lam_k2'][l].astype(f32))) + li)
    o = attend(l, q, k, v, lam)
    o = rmsnorm(o, P['subln_w'][l], SUBLN_EPS) * (1.0 - li)
    ya = o.reshape(B, T, ATT_W) @ P['w_att_br'][l]
    xc, conv_new = causal_conv(rx, conv_buf, P['conv_w'][l], P['conv_b'][l])
    hs, h_last = rg_lru(xc, h0, P['w_rg_a'][l], P['b_rg_a'][l], P['w_rg_x'][l], P['b_rg_x'][l],
                        P['lru_param'][l])
    yr = (hs * jax.nn.gelu(rg.astype(f32))).astype(x.dtype) @ P['w_rnn_br'][l]
    merged = jax.nn.sigmoid(ga) * ya + jax.nn.sigmoid(gr) * yr
    x = x + gt[:, 1] * (merged @ P['w_o'][l])
    h = rmsnorm(x, P['norm_w'][l, 2]) * (1.0 + sc[:, 2]) + sh[:, 2]
    x = x + 0.5 * gt[:, 2] * swiglu(h, P['ffn2_wi'][l], P['ffn2_wo'][l])
    return x, k, v, conv_new, h_last.astype(x.dtype)


def run_group(x, c, pos, conv_state, h_state, attend, P):
    ks, vs, convs, hs = [], [], [], []
    for l in range(DEPTH):
        x, k, v, cb, hl = layer(l, x, c, pos, conv_state[l], h_state[l], attend, P)
        ks.append(k)
        vs.append(v)
        convs.append(cb)
        hs.append(hl)
    y = rmsnorm(x, P['norm_f'])
    return y, jnp.stack(ks), jnp.stack(vs), jnp.stack(convs), jnp.stack(hs)


def setup_inputs(seed: int = 0) -> dict:
    key = jax.random.key(seed)
    ks = jax.random.split(key, 40)
    f32 = jnp.float32

    def nrm(k, shape, scale):
        return jax.random.normal(k, shape, f32) * scale

    n_pages = PAST_LEN // PAGE_SIZE
    n_used = DEC_BATCH * n_pages
    n_pool = n_used + max(1, n_used // 4)
    page_table = jax.random.permutation(ks[0], n_pool)[:n_used].reshape(DEC_BATCH, n_pages).astype(jnp.int32)

    u = jax.random.uniform(ks[1], (DEPTH, D_RNN), f32, minval=0.9, maxval=0.999)
    s = u ** (1.0 / LRU_C)
    lru_param = jnp.log(s) - jnp.log1p(-s)

    return {
        "x_prompt": nrm(ks[2], (BATCH, SEQ, D_MODEL), 1.0),
        "x_sample": nrm(ks[3], (DEC_BATCH, DEC_SEQ, D_MODEL), 1.0),
        "cache_k": nrm(ks[4], (DEPTH, n_pool, PAGE_SIZE, N_QK, HEAD_DIM), 1.0),
        "cache_v": nrm(ks[5], (DEPTH, n_pool, PAGE_SIZE, N_HEADS, 2 * HEAD_DIM), 1.0),
        "state_conv": nrm(ks[6], (DEPTH, DEC_BATCH, CONV_W - 1, D_RNN), 1.0),
        "state_h": nrm(ks[7], (DEPTH, DEC_BATCH, D_RNN), 0.5),
        "page_table": page_table,
        "c_prompt": nrm(ks[8], (BATCH, D_MODEL), 1.0),
        "c_sample": nrm(ks[9], (DEC_BATCH, D_MODEL), 1.0),
        "norm_w": 1.0 + nrm(ks[10], (DEPTH, 3, D_MODEL), 0.02),
        "w_ada": nrm(ks[11], (DEPTH, D_MODEL, 9 * D_MODEL), 0.5 * D_MODEL ** -0.5),
        "b_ada": nrm(ks[12], (DEPTH, 9 * D_MODEL), 0.02),
        "ffn1_wi": nrm(ks[13], (DEPTH, D_MODEL, 2 * D_FF), D_MODEL ** -0.5),
        "ffn1_wo": nrm(ks[14], (DEPTH, D_FF, D_MODEL), D_FF ** -0.5),
        "ffn2_wi": nrm(ks[15], (DEPTH, D_MODEL, 2 * D_FF), D_MODEL ** -0.5),
        "ffn2_wo": nrm(ks[16], (DEPTH, D_FF, D_MODEL), D_FF ** -0.5),
        "w_in": nrm(ks[17], (DEPTH, D_MODEL, IN_W), D_MODEL ** -0.5),
        "lam_q1": nrm(ks[18], (DEPTH, HEAD_DIM), 0.1),
        "lam_k1": nrm(ks[19], (DEPTH, HEAD_DIM), 0.1),
        "lam_q2": nrm(ks[20], (DEPTH, HEAD_DIM), 0.1),
        "lam_k2": nrm(ks[21], (DEPTH, HEAD_DIM), 0.1),
        "subln_w": 1.0 + nrm(ks[22], (DEPTH, 2 * HEAD_DIM), 0.02),
        "conv_w": nrm(ks[23], (DEPTH, CONV_W, D_RNN), CONV_W ** -0.5),
        "conv_b": nrm(ks[24], (DEPTH, D_RNN), 0.02),
        "w_rg_a": nrm(ks[25], (DEPTH, N_RNN_BLOCKS, RNN_BW, RNN_BW), RNN_BW ** -0.5),
        "b_rg_a": nrm(ks[26], (DEPTH, D_RNN), 0.02),
        "w_rg_x": nrm(ks[27], (DEPTH, N_RNN_BLOCKS, RNN_BW, RNN_BW), RNN_BW ** -0.5),
        "b_rg_x": nrm(ks[28], (DEPTH, D_RNN), 0.02),
        "lru_param": lru_param,
        "w_att_br": nrm(ks[29], (DEPTH, ATT_W, D_MODEL), ATT_W ** -0.5),
        "w_rnn_br": nrm(ks[30], (DEPTH, D_RNN, D_MODEL), D_RNN ** -0.5),
        "w_o": nrm(ks[31], (DEPTH, D_MODEL, D_MODEL), D_MODEL ** -0.5),
        "norm_f": 1.0 + nrm(ks[32], (D_MODEL,), 0.02),
    }


def reference(x_prompt, x_sample, cache_k, cache_v, state_conv, state_h, page_table, c_prompt, c_sample,
              norm_w, w_ada, b_ada, ffn1_wi, ffn1_wo, ffn2_wi, ffn2_wo, w_in, lam_q1, lam_k1, lam_q2, lam_k2,
              subln_w, conv_w, conv_b, w_rg_a, b_rg_a, w_rg_x, b_rg_x, lru_param, w_att_br, w_rnn_br, w_o,
              norm_f):
    P = dict(norm_w=norm_w, w_ada=w_ada, b_ada=b_ada, ffn1_wi=ffn1_wi, ffn1_wo=ffn1_wo, ffn2_wi=ffn2_wi,
             ffn2_wo=ffn2_wo, w_in=w_in, lam_q1=lam_q1, lam_k1=lam_k1, lam_q2=lam_q2, lam_k2=lam_k2,
             subln_w=subln_w, conv_w=conv_w, conv_b=conv_b, w_rg_a=w_rg_a, b_rg_a=b_rg_a, w_rg_x=w_rg_x,
             b_rg_x=b_rg_x, lru_param=lru_param, w_att_br=w_att_br, w_rnn_br=w_rnn_br, w_o=w_o, norm_f=norm_f)

    def prompt_attend(l, q, k, v, lam):
        B, T = q.shape[0], q.shape[1]
        nb = T // Q_BLOCK
        qb = q.reshape(B, nb, Q_BLOCK, N_QK, HEAD_DIM).swapaxes(0, 1)
        kpos = jnp.arange(T)

        def blk(args):
            qi, bi = args
            qpos = bi * Q_BLOCK + jnp.arange(Q_BLOCK)
            s = jnp.where(kpos[None, :] <= qpos[:, None], attn_scores(qi, k), NEG_INF)
            w = diff_weights(s, lam)
            return jnp.einsum('bhqk,bkhe->bqhe', w.astype(v.dtype), v)

        o = lax.map(blk, (qb, jnp.arange(nb)))
        return o.swapaxes(0, 1).reshape(B, T, N_HEADS, 2 * HEAD_DIM)

    def sample_attend(l, q, k, v, lam):
        B, T = q.shape[0], q.shape[1]
        kp = cache_k[l, page_table].reshape(B, -1, N_QK, HEAD_DIM)
        vp = cache_v[l, page_table].reshape(B, -1, N_HEADS, 2 * HEAD_DIM)
        plen = kp.shape[1]
        tri = jnp.arange(T)[None, :] <= jnp.arange(T)[:, None]
        s = jnp.concatenate([attn_scores(q, kp), jnp.where(tri, attn_scores(q, k), NEG_INF)], axis=-1)
        w = diff_weights(s, lam).astype(v.dtype)
        return (jnp.einsum('bhqk,bkhe->bqhe', w[..., :plen], vp.astype(v.dtype))
                + jnp.einsum('bhqk,bkhe->bqhe', w[..., plen:], v))

    Bp, Tp = x_prompt.shape[0], x_prompt.shape[1]
    pos_p = jnp.arange(Tp)
    conv0 = jnp.zeros((DEPTH, Bp, CONV_W - 1, D_RNN), x_prompt.dtype)
    h0 = jnp.zeros((DEPTH, Bp, D_RNN), x_prompt.dtype)
    y_prompt, k_prompt, v_prompt, conv_prompt, h_prompt = run_group(
        x_prompt, c_prompt, pos_p, conv0, h0, prompt_attend, P)

    past_len = page_table.shape[1] * PAGE_SIZE
    pos_s = past_len + jnp.arange(x_sample.shape[1])
    y_sample, k_sample, v_sample, conv_sample, h_sample = run_group(
        x_sample, c_sample, pos_s, state_conv, state_h, sample_attend, P)

    return (y_prompt, y_sample, k_prompt, v_prompt, conv_prompt, h_prompt,
            k_sample, v_sample, conv_sample, h_sample)
```

```python
import functools
import math

import jax
import jax.numpy as jnp
from jax import lax
from jax.experimental import pallas as pl
from jax.experimental.pallas import tpu as pltpu

F32 = jnp.float32
BF16 = jnp.bfloat16

D_MODEL = 1024
DEPTH = 4
N_HEADS = 8
HEAD_DIM = 64
N_QK = 2 * N_HEADS
V_DIM = 2 * HEAD_DIM
D_RNN = 1280
N_RNN_BLOCKS = 10
RNN_BW = D_RNN // N_RNN_BLOCKS
CONV_W = 4
LRU_C = 8.0
D_FF = 2816
ROPE_THETA = 10000.0
PAGE_SIZE = 128
NORM_EPS = 1e-6
SUBLN_EPS = 1e-5
NEG_INF = -1e30

V7X_VMEM_BYTES = 64 * 1024 * 1024
VMEM_LIMIT = V7X_VMEM_BYTES - 8 * 1024 * 1024

ROW_TILE = 512
FF_TILE = 1408
ATT_TILE = 512
RNN_TILE = 256
PAGES_PER_STEP = 8


def _lambda_init(l):
    return 0.8 - 0.6 * math.exp(-0.3 * l)


def _params(sem, vmem=VMEM_LIMIT):
    return pltpu.CompilerParams(dimension_semantics=sem, vmem_limit_bytes=vmem)


def _rms(x, eps):
    return x * lax.rsqrt(jnp.mean(x * x, axis=-1, keepdims=True) + eps)


def _modulated_norm(x, nw, sc, sh):
    return (_rms(x, NORM_EPS) * nw) * (1.0 + sc) + sh


def _lam(lam_ref, li):
    a = jnp.sum(lam_ref[0:1, :] * lam_ref[1:2, :], axis=-1, keepdims=True)
    b = jnp.sum(lam_ref[2:3, :] * lam_ref[3:4, :], axis=-1, keepdims=True)
    return jnp.exp(a) - jnp.exp(b) + li


def _ada_kernel(c_ref, w_ref, b_ref, o_ref):
    c = c_ref[...]
    s = (c * jax.nn.sigmoid(c)).astype(BF16)
    o_ref[...] = jnp.dot(s, w_ref[...].astype(BF16), preferred_element_type=F32) + b_ref[...]


def _ada(c_all, w_ada, b_ada):
    m = c_all.shape[0]
    n_col = w_ada.shape[2] // D_MODEL
    return pl.pallas_call(
        _ada_kernel,
        grid=(DEPTH, n_col),
        in_specs=[
            pl.BlockSpec((m, D_MODEL), lambda l, j: (0, 0)),
            pl.BlockSpec((None, D_MODEL, D_MODEL), lambda l, j: (l, 0, j)),
            pl.BlockSpec((None, 1, D_MODEL), lambda l, j: (l, 0, j)),
        ],
        out_specs=pl.BlockSpec((None, m, D_MODEL), lambda l, j: (l, 0, j)),
        out_shape=jax.ShapeDtypeStruct((DEPTH, m, w_ada.shape[2]), F32),
        compiler_params=_params(("arbitrary", "arbitrary")),
        name="ada",
    )(c_all, w_ada, b_ada.reshape(DEPTH, 1, -1))


def _mod_spec(ada, col, tiles_per_group):
    _, r, _ = ada.shape
    return pl.BlockSpec((None, r, D_MODEL), lambda i, *_: (i // tiles_per_group, 0, col))


def _ffn_kernel(x_ref, sh_ref, sc_ref, gt_ref, nw_ref, wg_ref, wu_ref, wo_ref, *rest, final):
    if final:
        nf_ref, o_ref, h_scr, acc_scr = rest
    else:
        o_ref, h_scr, acc_scr = rest
    j = pl.program_id(1)

    @pl.when(j == 0)
    def _():
        h = _modulated_norm(x_ref[...], nw_ref[...], sc_ref[...], sh_ref[...])
        h_scr[...] = h.astype(BF16)
        acc_scr[...] = jnp.zeros_like(acc_scr)

    h = h_scr[...]
    g = jnp.dot(h, wg_ref[...], preferred_element_type=F32)
    u = jnp.dot(h, wu_ref[...], preferred_element_type=F32)
    a = (g * jax.nn.sigmoid(g) * u).astype(BF16)
    acc_scr[...] += jnp.dot(a, wo_ref[...], preferred_element_type=F32)

    @pl.when(j == pl.num_programs(1) - 1)
    def _():
        y = x_ref[...] + (0.5 * gt_ref[...]) * acc_scr[...]
        if final:
            y = _rms(y, NORM_EPS) * nf_ref[...]
        o_ref[...] = y


def _ffn(x, ada, sub, nw, wi, wo, tiles_per_group, norm_f=None):
    n = x.shape[0]
    tm = min(ROW_TILE, n)
    n_f = D_FF // FF_TILE
    final = norm_f is not None
    row = pl.BlockSpec((tm, D_MODEL), lambda i, j: (i, 0))
    vec = pl.BlockSpec((1, D_MODEL), lambda i, j: (0, 0))
    in_specs = [
        row,
        _mod_spec(ada, 3 * sub + 0, tiles_per_group),
        _mod_spec(ada, 3 * sub + 1, tiles_per_group),
        _mod_spec(ada, 3 * sub + 2, tiles_per_group),
        vec,
        pl.BlockSpec((D_MODEL, FF_TILE), lambda i, j: (0, j)),
        pl.BlockSpec((D_MODEL, FF_TILE), lambda i, j: (0, j + n_f)),
        pl.BlockSpec((FF_TILE, D_MODEL), lambda i, j: (j, 0)),
    ]
    args = [x, ada, ada, ada, nw.reshape(1, D_MODEL), wi, wi, wo]
    if final:
        in_specs.append(vec)
        args.append(norm_f.reshape(1, D_MODEL))
    return pl.pallas_call(
        functools.partial(_ffn_kernel, final=final),
        grid=(n // tm, n_f),
        in_specs=in_specs,
        out_specs=row,
        out_shape=jax.ShapeDtypeStruct((n, D_MODEL), F32),
        scratch_shapes=[pltpu.VMEM((tm, D_MODEL), BF16), pltpu.VMEM((tm, D_MODEL), F32)],
        compiler_params=_params(("arbitrary", "arbitrary")),
        name="ffn",
    )(*args)


def _rope_store(o_ref, r, cos, sin, scale):
    lane = lax.broadcasted_iota(jnp.int32, cos.shape, 1)
    first = (lane % HEAD_DIM) < (HEAD_DIM // 2)
    for c in range(D_MODEL // 128):
        blk = r[:, c * 128:(c + 1) * 128]
        swapped = jnp.where(first, pltpu.roll(blk, 128 - HEAD_DIM // 2, 1),
                            pltpu.roll(blk, HEAD_DIM // 2, 1))
        out = blk * cos + swapped * sin
        if scale != 1.0:
            out = out * scale
        o_ref[:, c * 128:(c + 1) * 128] = out.astype(o_ref.dtype)


def _inproj_kernel(x_ref, sh_ref, sc_ref, nw_ref, cos_ref, sin_ref, wa_ref, wb_ref,
                   q_ref, k_ref, v_ref, ga_ref, gr_ref, rx_ref, rg_ref, h_scr):
    j = pl.program_id(1)

    @pl.when(j == 0)
    def _():
        h = _modulated_norm(x_ref[...], nw_ref[...], sc_ref[...], sh_ref[...])
        h_scr[...] = h.astype(BF16)

    def proj(w_ref):
        return jnp.dot(h_scr[...], w_ref[...], preferred_element_type=F32)

    @pl.when(j == 0)
    def _():
        _rope_store(q_ref, proj(wa_ref), cos_ref[...], sin_ref[...], HEAD_DIM ** -0.5)

    @pl.when(j == 1)
    def _():
        _rope_store(k_ref, proj(wa_ref), cos_ref[...], sin_ref[...], 1.0)

    for step, ref in ((2, v_ref), (3, ga_ref), (4, gr_ref)):
        @pl.when(j == step)
        def _(ref=ref):
            ref[...] = proj(wa_ref)

    for step, ref in ((5, rx_ref), (6, rg_ref)):
        @pl.when(j == step)
        def _(ref=ref):
            ref[...] = proj(wb_ref)


def _inproj(x, ada, nw, cos, sin, wa, wb, tiles_per_group):
    n = x.shape[0]
    tm = min(ROW_TILE, n)
    n_pos_tiles = cos.shape[0] // tm
    n_a, n_b = wa.shape[0], wb.shape[0]
    row = pl.BlockSpec((tm, D_MODEL), lambda i, j: (i, 0))
    row_r = pl.BlockSpec((tm, D_RNN), lambda i, j: (i, 0))
    tab = pl.BlockSpec((tm, 128), lambda i, j: (i % n_pos_tiles, 0))
    return pl.pallas_call(
        _inproj_kernel,
        grid=(n // tm, n_a + n_b),
        in_specs=[
            row,
            _mod_spec(ada, 3, tiles_per_group),
            _mod_spec(ada, 4, tiles_per_group),
            pl.BlockSpec((1, D_MODEL), lambda i, j: (0, 0)),
            tab, tab,
            pl.BlockSpec((None, D_MODEL, D_MODEL), lambda i, j: (jnp.minimum(j, n_a - 1), 0, 0)),
            pl.BlockSpec((None, D_MODEL, D_RNN), lambda i, j: (jnp.maximum(j - n_a, 0), 0, 0)),
        ],
        out_specs=[row, row, row, row, row, row_r, row_r],
        out_shape=[jax.ShapeDtypeStruct((n, D_MODEL), BF16)]
        + [jax.ShapeDtypeStruct((n, D_MODEL), F32)] * 4
        + [jax.ShapeDtypeStruct((n, D_RNN), F32)] * 2,
        scratch_shapes=[pltpu.VMEM((tm, D_MODEL), BF16)],
        compiler_params=_params(("arbitrary", "arbitrary")),
        name="inproj",
    )(x, ada, ada, nw.reshape(1, D_MODEL), cos, sin, wa, wb)


def _subln(o1, l1, o2, l2, lam, w, li):
    o = o1 / l1 - lam * (o2 / l2)
    return (_rms(o, SUBLN_EPS) * w) * (1.0 - li)


def _flash_kernel(qi_tab, ki_tab, q_ref, k_ref, v_ref, lam_ref, w_ref, o_ref,
                  m1, l1, a1, m2, l2, a2, *, li):
    p = pl.program_id(2)
    qi = qi_tab[p]
    ki = ki_tab[p]

    @pl.when(ki == 0)
    def _():
        for m, l, a in ((m1, l1, a1), (m2, l2, a2)):
            m[...] = jnp.full_like(m, NEG_INF)
            l[...] = jnp.zeros_like(l)
            a[...] = jnp.zeros_like(a)

    q = q_ref[...]
    k = k_ref[...].astype(BF16)
    v = v_ref[...].astype(BF16)
    lane = lax.broadcasted_iota(jnp.int32, q.shape, 1)
    tq, tk = q.shape[0], k.shape[0]
    row = lax.broadcasted_iota(jnp.int32, (tq, tk), 0)
    col = lax.broadcasted_iota(jnp.int32, (tq, tk), 1)
    visible = (ki < qi) | (col <= row)
    nt = (((1,), (1,)), ((), ()))
    for sel, m_ref, l_ref, a_ref in ((lane < HEAD_DIM, m1, l1, a1), (lane >= HEAD_DIM, m2, l2, a2)):
        s = lax.dot_general(jnp.where(sel, q, jnp.zeros_like(q)), k, nt, preferred_element_type=F32)
        s = jnp.where(visible, s, NEG_INF)
        m_old = m_ref[:, 0:1]
        m_new = jnp.maximum(m_old, jnp.max(s, axis=-1, keepdims=True))
        alpha = jnp.exp(m_old - m_new)
        e = jnp.exp(s - m_new)
        l_ref[...] = jnp.broadcast_to(alpha * l_ref[:, 0:1] + jnp.sum(e, axis=-1, keepdims=True), l_ref.shape)
        a_ref[...] = alpha * a_ref[...] + jnp.dot(e.astype(BF16), v, preferred_element_type=F32)
        m_ref[...] = jnp.broadcast_to(m_new, m_ref.shape)

    @pl.when(ki == qi)
    def _():
        o_ref[...] = _subln(a1[...], l1[:, 0:1], a2[...], l2[:, 0:1], _lam(lam_ref, li), w_ref[...], li)


def _flash(q, k, v, lam_rows, subln_w, batch, seq, li):
    t = ATT_TILE
    nq = seq // t
    pairs = [(a, b) for a in range(nq) for b in range(a + 1)]
    qi_tab = jnp.asarray([a for a, _ in pairs], jnp.int32)
    ki_tab = jnp.asarray([b for _, b in pairs], jnp.int32)
    qspec = pl.BlockSpec((t, 128), lambda b, h, p, qt, kt: (b * nq + qt[p], h))
    kspec = pl.BlockSpec((t, 128), lambda b, h, p, qt, kt: (b * nq + kt[p], h))
    stat = pltpu.VMEM((t, 128), F32)
    return pl.pallas_call(
        functools.partial(_flash_kernel, li=li),
        grid_spec=pltpu.PrefetchScalarGridSpec(
            num_scalar_prefetch=2,
            grid=(batch, N_HEADS, len(pairs)),
            in_specs=[qspec, kspec, kspec,
                      pl.BlockSpec((4, HEAD_DIM), lambda b, h, p, qt, kt: (0, 0)),
                      pl.BlockSpec((1, V_DIM), lambda b, h, p, qt, kt: (0, 0))],
            out_specs=qspec,
            scratch_shapes=[stat] * 6,
        ),
        out_shape=jax.ShapeDtypeStruct((batch * seq, D_MODEL), F32),
        compiler_params=_params(("arbitrary", "arbitrary", "arbitrary")),
        name="flash",
    )(qi_tab, ki_tab, q, k, v, lam_rows, subln_w.reshape(1, V_DIM))


def _paged_kernel(pt_ref, qs_ref, kn_ref, vn_ref, lam_ref, w_ref, *rest, li):
    npg = PAGES_PER_STEP
    k_refs, v_refs = rest[:npg], rest[npg:2 * npg]
    o_ref, m_scr, l_scr, a_scr = rest[2 * npg:]
    hf = pl.program_id(1)
    qs = qs_ref[...]
    nt = (((1,), (1,)), ((), ()))
    n_rows = 2 * 4 * N_HEADS
    half = n_rows // 2

    def scores(k_of_map):
        parts = [lax.dot_general(qs[j * half:(j + 1) * half], k_of_map(j), nt, preferred_element_type=F32)
                 for j in range(2)]
        return jnp.concatenate(parts, axis=0)

    @pl.when(hf == 0)
    def _():
        s = scores(lambda j: kn_ref[j].astype(BF16))
        row = lax.broadcasted_iota(jnp.int32, s.shape, 0)
        col = lax.broadcasted_iota(jnp.int32, s.shape, 1)
        ok = ((row % N_HEADS) == (col % N_HEADS)) & ((col // N_HEADS) <= ((row % half) // N_HEADS))
        s = jnp.where(ok, s, NEG_INF)
        m = jnp.max(s, axis=-1, keepdims=True)
        e = jnp.exp(s - m)
        m_scr[...] = jnp.broadcast_to(m, m_scr.shape)
        l_scr[...] = jnp.broadcast_to(jnp.sum(e, axis=-1, keepdims=True), l_scr.shape)
        a_scr[...] = jnp.dot(e.astype(BF16), vn_ref[...].astype(BF16), preferred_element_type=F32)

    def page_keys(p):
        def of_map(j):
            kj = k_refs[p][:, pl.ds(j, N_HEADS, stride=2), :]
            return kj.reshape(PAGE_SIZE * N_HEADS, HEAD_DIM).astype(BF16)
        return of_map

    s = jnp.concatenate([scores(page_keys(p)) for p in range(npg)], axis=1)
    row = lax.broadcasted_iota(jnp.int32, s.shape, 0)
    col = lax.broadcasted_iota(jnp.int32, s.shape, 1)
    s = jnp.where((row % N_HEADS) == (col % N_HEADS), s, NEG_INF)
    m_old = m_scr[:, 0:1]
    m_new = jnp.maximum(m_old, jnp.max(s, axis=-1, keepdims=True))
    alpha = jnp.exp(m_old - m_new)
    e = jnp.exp(s - m_new).astype(BF16)
    l_scr[...] = jnp.broadcast_to(alpha * l_scr[:, 0:1] + jnp.sum(e.astype(F32), axis=-1, keepdims=True),
                                  l_scr.shape)
    acc = alpha * a_scr[...]
    width = PAGE_SIZE * N_HEADS
    for p in range(npg):
        vp = v_refs[p][...].reshape(width, V_DIM).astype(BF16)
        acc = acc + jnp.dot(e[:, p * width:(p + 1) * width], vp, preferred_element_type=F32)
    a_scr[...] = acc
    m_scr[...] = jnp.broadcast_to(m_new, m_scr.shape)

    @pl.when(hf == pl.num_programs(1) - 1)
    def _():
        a = a_scr[...]
        l = l_scr[:, 0:1]
        o_ref[...] = _subln(a[:half], l[:half], a[half:], l[half:], _lam(lam_ref, li), w_ref[...], li)


def _paged(page_table, qs, kn, vn, lam_rows, subln_w, cache_k, cache_v, layer, li):
    batch, n_pages = page_table.shape
    npg = PAGES_PER_STEP
    steps = n_pages // npg

    def page_spec(shape, p):
        return pl.BlockSpec((None, None) + shape,
                            lambda b, hf, pt: (layer, pt[b * n_pages + hf * npg + p], 0, 0, 0))

    n_rows = 2 * 4 * N_HEADS
    stat = pltpu.VMEM((n_rows, 128), F32)
    return pl.pallas_call(
        functools.partial(_paged_kernel, li=li),
        grid_spec=pltpu.PrefetchScalarGridSpec(
            num_scalar_prefetch=1,
            grid=(batch, steps),
            in_specs=[
                pl.BlockSpec((None, n_rows, HEAD_DIM), lambda b, hf, pt: (b, 0, 0)),
                pl.BlockSpec((None, 2, 4 * N_HEADS, HEAD_DIM), lambda b, hf, pt: (b, 0, 0, 0)),
                pl.BlockSpec((None, 4 * N_HEADS, V_DIM), lambda b, hf, pt: (b, 0, 0)),
                pl.BlockSpec((4, HEAD_DIM), lambda b, hf, pt: (0, 0)),
                pl.BlockSpec((1, V_DIM), lambda b, hf, pt: (0, 0)),
            ]
            + [page_spec((PAGE_SIZE, N_QK, HEAD_DIM), p) for p in range(npg)]
            + [page_spec((PAGE_SIZE, N_HEADS, V_DIM), p) for p in range(npg)],
            out_specs=pl.BlockSpec((None, 4 * N_HEADS, V_DIM), lambda b, hf, pt: (b, 0, 0)),
            scratch_shapes=[stat, stat, stat],
        ),
        out_shape=jax.ShapeDtypeStruct((batch, 4 * N_HEADS, V_DIM), F32),
        compiler_params=_params(("arbitrary", "arbitrary")),
        name="paged",
    )(page_table.reshape(-1), qs, kn, vn, lam_rows, subln_w.reshape(1, V_DIM),
      *([cache_k] * npg), *([cache_v] * npg))


def _log_sigmoid(x):
    return -(jnp.maximum(-x, 0.0) + jnp.log1p(jnp.exp(-jnp.abs(x))))


def _gelu_tanh(x):
    return 0.5 * x * (1.0 + jnp.tanh(math.sqrt(2.0 / math.pi) * (x + 0.044715 * (x * x * x))))


def _lru_coeffs(xc, wa_ref, ba, wx_ref, bx, lru):
    ra, ri = [], []
    for n in range(N_RNN_BLOCKS):
        xb = xc[:, n * RNN_BW:(n + 1) * RNN_BW].astype(BF16)
        ra.append(jnp.dot(xb, wa_ref[n], preferred_element_type=F32))
        ri.append(jnp.dot(xb, wx_ref[n], preferred_element_type=F32))
    r = jax.nn.sigmoid(jnp.concatenate(ra, axis=1) + ba)
    i = jax.nn.sigmoid(jnp.concatenate(ri, axis=1) + bx)
    log_a = LRU_C * r * _log_sigmoid(lru)
    a = jnp.exp(log_a)
    b = jnp.sqrt(-jnp.tanh(log_a) * (a * a + 1.0)) * (i * xc)
    return a, b


def _rnn_kernel(rx_ref, rg_ref, cw_ref, cb_ref, wa_ref, ba_ref, wx_ref, bx_ref, lru_ref,
                hg_ref, hl_ref, xext, a_scr, b_scr, h_scr, carry):
    t = pl.program_id(1)
    tc = rx_ref.shape[0]

    @pl.when(t == 0)
    def _():
        xext[0:8, :] = jnp.zeros((8, D_RNN), F32)
        carry[...] = jnp.zeros_like(carry)

    x = rx_ref[...]
    xext[8:8 + tc, :] = x
    xc = cb_ref[...] + cw_ref[3:4, :] * x
    for j in range(CONV_W - 1):
        xc = xc + cw_ref[j:j + 1, :] * xext[5 + j:5 + j + tc, :]
    xext[0:8, :] = x[tc - 8:tc, :]

    a, b = _lru_coeffs(xc, wa_ref, ba_ref[...], wx_ref, bx_ref[...], lru_ref[...])
    sub = lax.broadcasted_iota(jnp.int32, a.shape, 0) % 8
    for d in (1, 2, 4):
        keep = sub >= d
        a_sh = jnp.where(keep, pltpu.roll(a, d, 0), 1.0)
        b_sh = jnp.where(keep, pltpu.roll(b, d, 0), 0.0)
        b = a * b_sh + b
        a = a * a_sh
    a_scr[...] = a
    b_scr[...] = b

    def group(g, c):
        r = pl.multiple_of(g * 8, 8)
        h = a_scr[pl.ds(r, 8), :] * c + b_scr[pl.ds(r, 8), :]
        h_scr[pl.ds(r, 8), :] = h
        return jnp.broadcast_to(h[7:8, :], (8, D_RNN))

    c = lax.fori_loop(0, tc // 8, group, carry[...])
    carry[...] = c
    hg_ref[...] = (h_scr[...] * _gelu_tanh(rg_ref[...])).astype(BF16)

    @pl.when(t == pl.num_programs(1) - 1)
    def _():
        hl_ref[...] = c[0:1, :]


def _rnn_weight_specs():
    def full(shape):
        return pl.BlockSpec(shape, lambda *_: (0,) * len(shape))

    return [full((CONV_W, D_RNN)), full((1, D_RNN)),
            full((N_RNN_BLOCKS, RNN_BW, RNN_BW)), full((1, D_RNN)),
            full((N_RNN_BLOCKS, RNN_BW, RNN_BW)), full((1, D_RNN)), full((1, D_RNN))]


def _rnn(rx, rg, rnn_w, batch, seq):
    tc = RNN_TILE
    nt = seq // tc
    row = pl.BlockSpec((tc, D_RNN), lambda b, t: (b * nt + t, 0))
    big = pltpu.VMEM((tc, D_RNN), F32)
    return pl.pallas_call(
        _rnn_kernel,
        grid=(batch, nt),
        in_specs=[row, row] + _rnn_weight_specs(),
        out_specs=[row, pl.BlockSpec((None, 1, D_RNN), lambda b, t: (b, 0, 0))],
        out_shape=[jax.ShapeDtypeStruct((batch * seq, D_RNN), BF16),
                   jax.ShapeDtypeStruct((batch, 1, D_RNN), F32)],
        scratch_shapes=[pltpu.VMEM((tc + 8, D_RNN), F32), big, big, big, pltpu.VMEM((8, D_RNN), F32)],
        compiler_params=_params(("arbitrary", "arbitrary")),
        name="rnn",
    )(rx, rg, *rnn_w)


def _rnn_step_kernel(rx_ref, rg_ref, cs_ref, h0_ref, cw_ref, cb_ref, wa_ref, ba_ref, wx_ref, bx_ref, lru_ref,
                     hg_ref, hl_ref):
    steps = rx_ref.shape[0]
    xs = [cs_ref[j] for j in range(CONV_W - 1)] + [rx_ref[t] for t in range(steps)]
    h = h0_ref[...]
    for t in range(steps):
        xc = cb_ref[...] + sum(cw_ref[j:j + 1, :] * xs[t + j] for j in range(CONV_W))
        a, b = _lru_coeffs(xc, wa_ref, ba_ref[...], wx_ref, bx_ref[...], lru_ref[...])
        h = a * h + b
        hg_ref[t] = (h * _gelu_tanh(rg_ref[t])).astype(BF16)
    hl_ref[...] = h


def _rnn_step(rx_t, rg_t, cs_t, h0, rnn_w):
    steps, batch, _ = rx_t.shape

    def full(shape):
        return pl.BlockSpec(shape, lambda i: (0,) * len(shape))

    return pl.pallas_call(
        _rnn_step_kernel,
        grid=(1,),
        in_specs=[full(rx_t.shape), full(rg_t.shape), full(cs_t.shape), full(h0.shape)] + _rnn_weight_specs(),
        out_specs=[full(rx_t.shape), full(h0.shape)],
        out_shape=[jax.ShapeDtypeStruct(rx_t.shape, BF16), jax.ShapeDtypeStruct(h0.shape, F32)],
        compiler_params=_params(("arbitrary",)),
        name="rnn_step",
    )(rx_t, rg_t, cs_t, h0, *rnn_w)


def _merge_kernel(x_ref, o_ref, hg_ref, ga_ref, gr_ref, gt_ref, watt_ref, wrnn_ref, wo_ref, y_ref):
    ya = jnp.dot(o_ref[...].astype(BF16), watt_ref[...], preferred_element_type=F32)
    yr = jnp.dot(hg_ref[...], wrnn_ref[...], preferred_element_type=F32)
    merged = jax.nn.sigmoid(ga_ref[...]) * ya + jax.nn.sigmoid(gr_ref[...]) * yr
    y_ref[...] = x_ref[...] + gt_ref[...] * jnp.dot(merged.astype(BF16), wo_ref[...],
                                                     preferred_element_type=F32)


def _merge(x, o, hg, ga, gr, ada, w_att, w_rnn, w_o, tiles_per_group):
    n = x.shape[0]
    tm = min(ROW_TILE, n)
    row = pl.BlockSpec((tm, D_MODEL), lambda i: (i, 0))

    def full(shape):
        return pl.BlockSpec(shape, lambda i: (0, 0))

    return pl.pallas_call(
        _merge_kernel,
        grid=(n // tm,),
        in_specs=[row, row, pl.BlockSpec((tm, D_RNN), lambda i: (i, 0)), row, row,
                  _mod_spec(ada, 5, tiles_per_group),
                  full((D_MODEL, D_MODEL)), full((D_RNN, D_MODEL)), full((D_MODEL, D_MODEL))],
        out_specs=row,
        out_shape=jax.ShapeDtypeStruct((n, D_MODEL), F32),
        compiler_params=_params(("arbitrary",)),
        name="merge",
    )(x, o, hg, ga, gr, ada, w_att, w_rnn, w_o)


def _rope_tables(pos):
    half = HEAD_DIM // 2
    inv = 1.0 / (ROPE_THETA ** (jnp.arange(half, dtype=F32) * 2.0 / HEAD_DIM))
    ang = pos.astype(F32)[:, None] * inv[None, :]
    cos, sin = jnp.cos(ang), jnp.sin(ang)
    return jnp.tile(jnp.concatenate([cos, cos], axis=1), (1, 2)), jnp.tile(jnp.concatenate([-sin, sin], axis=1), (1, 2))


def kernel(x_prompt, x_sample, cache_k, cache_v, state_conv, state_h, page_table, c_prompt, c_sample, norm_w, w_ada, b_ada, ffn1_wi, ffn1_wo, ffn2_wi, ffn2_wo, w_in, lam_q1, lam_k1, lam_q2, lam_k2, subln_w, conv_w, conv_b, w_rg_a, b_rg_a, w_rg_x, b_rg_x, lru_param, w_att_br, w_rnn_br, w_o, norm_f):
    bp, tp, _ = x_prompt.shape
    bs, ts, _ = x_sample.shape
    past_len = page_table.shape[1] * PAGE_SIZE
    assert tp % ATT_TILE == 0 and tp % ROW_TILE == 0 and tp % RNN_TILE == 0
    assert page_table.shape[1] % PAGES_PER_STEP == 0 and ts == 4

    ada = _ada(jnp.concatenate([c_prompt, c_sample], axis=0), w_ada, b_ada)
    ada_p = ada[:, :bp, None, :]
    ada_s = jnp.repeat(ada[:, bp:], ts, axis=1)[:, None]
    tiles_per_seq = tp // ROW_TILE

    cos_p, sin_p = _rope_tables(jnp.arange(tp))
    cos_s, sin_s = _rope_tables(jnp.tile(past_len + jnp.arange(ts), bs))

    xp = x_prompt.reshape(bp * tp, D_MODEL)
    xs = x_sample.reshape(bs * ts, D_MODEL)
    outs_p = [[] for _ in range(4)]
    outs_s = [[] for _ in range(4)]
    s_q, s_k, s_v = D_MODEL, 2 * D_MODEL, 3 * D_MODEL
    s_rx, s_rg, s_ga = s_v + D_RNN, s_v + 2 * D_RNN, s_v + 2 * D_RNN + D_MODEL

    for l in range(DEPTH):
        li = _lambda_init(l)
        bf = lambda w: w.astype(BF16)
        wi1, wo1, wi2, wo2 = bf(ffn1_wi[l]), bf(ffn1_wo[l]), bf(ffn2_wi[l]), bf(ffn2_wo[l])
        wl = w_in[l]
        wa = bf(jnp.stack([wl[:, :s_q], wl[:, s_q:s_k], wl[:, s_k:s_v], wl[:, s_rg:s_ga], wl[:, s_ga:]]))
        wb = bf(jnp.stack([wl[:, s_v:s_rx], wl[:, s_rx:s_rg]]))
        w_att, w_rnn, w_out = bf(w_att_br[l]), bf(w_rnn_br[l]), bf(w_o[l])
        rnn_w = (conv_w[l], conv_b[l][None], bf(w_rg_a[l]), b_rg_a[l][None], bf(w_rg_x[l]), b_rg_x[l][None],
                 lru_param[l][None])
        lam_rows = jnp.stack([lam_q1[l], lam_k1[l], lam_q2[l], lam_k2[l]])
        last = l == DEPTH - 1

        xp = _ffn(xp, ada_p[l], 0, norm_w[l, 0], wi1, wo1, tiles_per_seq)
        q, k, v, ga, gr, rx, rg = _inproj(xp, ada_p[l], norm_w[l, 1], cos_p, sin_p, wa, wb, tiles_per_seq)
        o = _flash(q, k, v, lam_rows, subln_w[l], bp, tp, li)
        hg, h_last = _rnn(rx, rg, rnn_w, bp, tp)
        xp = _merge(xp, o, hg, ga, gr, ada_p[l], w_att, w_rnn, w_out, tiles_per_seq)
        xp = _ffn(xp, ada_p[l], 2, norm_w[l, 2], wi2, wo2, tiles_per_seq, norm_f if last else None)
        outs_p[0].append(k.reshape(bp, tp, N_QK, HEAD_DIM))
        outs_p[1].append(v.reshape(bp, tp, N_HEADS, V_DIM))
        outs_p[2].append(rx.reshape(bp, tp, D_RNN)[:, tp - (CONV_W - 1):])
        outs_p[3].append(h_last.reshape(bp, D_RNN))

        xs = _ffn(xs, ada_s[l], 0, norm_w[l, 0], wi1, wo1, 1)
        q, k, v, ga, gr, rx, rg = _inproj(xs, ada_s[l], norm_w[l, 1], cos_s, sin_s, wa, wb, 1)
        qs = q.reshape(bs, ts, N_HEADS, 2, HEAD_DIM).transpose(0, 3, 1, 2, 4).reshape(bs, 2 * ts * N_HEADS, HEAD_DIM)
        kn = k.reshape(bs, ts, N_HEADS, 2, HEAD_DIM).transpose(0, 3, 1, 2, 4).reshape(bs, 2, ts * N_HEADS, HEAD_DIM)
        vn = v.reshape(bs, ts * N_HEADS, V_DIM)
        o = _paged(page_table, qs, kn, vn, lam_rows, subln_w[l], cache_k, cache_v, l, li)
        o = o.reshape(bs * ts, D_MODEL)
        tm = lambda a: a.reshape(bs, ts, D_RNN).transpose(1, 0, 2)
        cs_t = jnp.concatenate([state_conv[l].transpose(1, 0, 2), tm(rx)], axis=0)
        hg_t, h_last = _rnn_step(tm(rx), tm(rg), cs_t[:CONV_W - 1], state_h[l], rnn_w)
        hg = hg_t.transpose(1, 0, 2).reshape(bs * ts, D_RNN)
        xs = _merge(xs, o, hg, ga, gr, ada_s[l], w_att, w_rnn, w_out, 1)
        xs = _ffn(xs, ada_s[l], 2, norm_w[l, 2], wi2, wo2, 1, norm_f if last else None)
        outs_s[0].append(k.reshape(bs, ts, N_QK, HEAD_DIM))
        outs_s[1].append(v.reshape(bs, ts, N_HEADS, V_DIM))
        outs_s[2].append(cs_t[ts:].transpose(1, 0, 2))
        outs_s[3].append(h_last)

    stack = lambda xs_: jnp.stack(xs_)
    return (xp.reshape(bp, tp, D_MODEL), xs.reshape(bs, ts, D_MODEL),
            stack(outs_p[0]), stack(outs_p[1]), stack(outs_p[2]), stack(outs_p[3]),
            stack(outs_s[0]), stack(outs_s[1]), stack(outs_s[2]), stack(outs_s[3]))
```

```python
import functools
import math

import jax
import jax.numpy as jnp
from jax import lax
from jax.experimental import pallas as pl
from jax.experimental.pallas import tpu as pltpu

F32 = jnp.float32
BF16 = jnp.bfloat16

D_MODEL = 1024
DEPTH = 4
N_HEADS = 8
HEAD_DIM = 64
N_QK = 2 * N_HEADS
V_DIM = 2 * HEAD_DIM
D_RNN = 1280
N_RNN_BLOCKS = 10
RNN_BW = D_RNN // N_RNN_BLOCKS
CONV_W = 4
LRU_C = 8.0
D_FF = 2816
ROPE_THETA = 10000.0
PAGE_SIZE = 128
NORM_EPS = 1e-6
SUBLN_EPS = 1e-5
NEG_INF = -1e30

V7X_VMEM_BYTES = 64 * 1024 * 1024
VMEM_LIMIT = V7X_VMEM_BYTES - 8 * 1024 * 1024

ROW_TILE = 512
FF_TILE = 1408
ATT_TILE = 256
RNN_TILE = 256
PAGES_PER_STEP = 8


def _lambda_init(l):
    return 0.8 - 0.6 * math.exp(-0.3 * l)


def _params(sem, vmem=VMEM_LIMIT):
    return pltpu.CompilerParams(dimension_semantics=sem, vmem_limit_bytes=vmem)


def _rms(x, eps):
    return x * lax.rsqrt(jnp.mean(x * x, axis=-1, keepdims=True) + eps)


def _modulated_norm(x, nw, sc, sh):
    return (_rms(x, NORM_EPS) * nw) * (1.0 + sc) + sh


def _lam(lam_ref, li):
    a = jnp.sum(lam_ref[0:1, :] * lam_ref[1:2, :], axis=-1, keepdims=True)
    b = jnp.sum(lam_ref[2:3, :] * lam_ref[3:4, :], axis=-1, keepdims=True)
    return jnp.exp(a) - jnp.exp(b) + li


def _ada_kernel(c_ref, w_ref, b_ref, o_ref):
    c = c_ref[...]
    s = (c * jax.nn.sigmoid(c)).astype(BF16)
    o_ref[...] = jnp.dot(s, w_ref[...].astype(BF16), preferred_element_type=F32) + b_ref[...]


def _ada(c_all, w_ada, b_ada):
    m = c_all.shape[0]
    n_col = w_ada.shape[2] // D_MODEL
    return pl.pallas_call(
        _ada_kernel,
        grid=(DEPTH, n_col),
        in_specs=[
            pl.BlockSpec((m, D_MODEL), lambda l, j: (0, 0)),
            pl.BlockSpec((None, D_MODEL, D_MODEL), lambda l, j: (l, 0, j)),
            pl.BlockSpec((None, 1, D_MODEL), lambda l, j: (l, 0, j)),
        ],
        out_specs=pl.BlockSpec((None, m, D_MODEL), lambda l, j: (l, 0, j)),
        out_shape=jax.ShapeDtypeStruct((DEPTH, m, w_ada.shape[2]), F32),
        compiler_params=_params(("arbitrary", "arbitrary")),
        name="ada",
    )(c_all, w_ada, b_ada.reshape(DEPTH, 1, -1))


def _mod_spec(ada, col, tiles_per_group):
    _, r, _ = ada.shape
    return pl.BlockSpec((None, r, D_MODEL), lambda i, *_: (i // tiles_per_group, 0, col))


def _ffn_kernel(x_ref, sh_ref, sc_ref, gt_ref, nw_ref, wg_ref, wu_ref, wo_ref, *rest, final):
    if final:
        nf_ref, o_ref, h_scr, acc_scr = rest
    else:
        o_ref, h_scr, acc_scr = rest
    j = pl.program_id(1)

    @pl.when(j == 0)
    def _():
        h = _modulated_norm(x_ref[...], nw_ref[...], sc_ref[...], sh_ref[...])
        h_scr[...] = h.astype(BF16)
        acc_scr[...] = jnp.zeros_like(acc_scr)

    h = h_scr[...]
    g = jnp.dot(h, wg_ref[...], preferred_element_type=F32)
    u = jnp.dot(h, wu_ref[...], preferred_element_type=F32)
    a = (g * jax.nn.sigmoid(g) * u).astype(BF16)
    acc_scr[...] += jnp.dot(a, wo_ref[...], preferred_element_type=F32)

    @pl.when(j == pl.num_programs(1) - 1)
    def _():
        y = x_ref[...] + (0.5 * gt_ref[...]) * acc_scr[...]
        if final:
            y = _rms(y, NORM_EPS) * nf_ref[...]
        o_ref[...] = y


def _ffn(x, ada, sub, nw, wi, wo, tiles_per_group, norm_f=None):
    n = x.shape[0]
    tm = min(ROW_TILE, n)
    n_f = D_FF // FF_TILE
    final = norm_f is not None
    row = pl.BlockSpec((tm, D_MODEL), lambda i, j: (i, 0))
    vec = pl.BlockSpec((1, D_MODEL), lambda i, j: (0, 0))
    in_specs = [
        row,
        _mod_spec(ada, 3 * sub + 0, tiles_per_group),
        _mod_spec(ada, 3 * sub + 1, tiles_per_group),
        _mod_spec(ada, 3 * sub + 2, tiles_per_group),
        vec,
        pl.BlockSpec((D_MODEL, FF_TILE), lambda i, j: (0, j)),
        pl.BlockSpec((D_MODEL, FF_TILE), lambda i, j: (0, j + n_f)),
        pl.BlockSpec((FF_TILE, D_MODEL), lambda i, j: (j, 0)),
    ]
    args = [x, ada, ada, ada, nw.reshape(1, D_MODEL), wi, wi, wo]
    if final:
        in_specs.append(vec)
        args.append(norm_f.reshape(1, D_MODEL))
    return pl.pallas_call(
        functools.partial(_ffn_kernel, final=final),
        grid=(n // tm, n_f),
        in_specs=in_specs,
        out_specs=row,
        out_shape=jax.ShapeDtypeStruct((n, D_MODEL), F32),
        scratch_shapes=[pltpu.VMEM((tm, D_MODEL), BF16), pltpu.VMEM((tm, D_MODEL), F32)],
        compiler_params=_params(("arbitrary", "arbitrary")),
        name="ffn",
    )(*args)


def _rope_store(o_ref, r, cos, sin, scale):
    lane = lax.broadcasted_iota(jnp.int32, cos.shape, 1)
    first = (lane % HEAD_DIM) < (HEAD_DIM // 2)
    for c in range(D_MODEL // 128):
        blk = r[:, c * 128:(c + 1) * 128]
        swapped = jnp.where(first, pltpu.roll(blk, 128 - HEAD_DIM // 2, 1),
                            pltpu.roll(blk, HEAD_DIM // 2, 1))
        out = blk * cos + swapped * sin
        if scale != 1.0:
            out = out * scale
        o_ref[:, c * 128:(c + 1) * 128] = out.astype(o_ref.dtype)


def _inproj_kernel(x_ref, sh_ref, sc_ref, nw_ref, cos_ref, sin_ref, wa_ref, wb_ref,
                   q_ref, k_ref, v_ref, ga_ref, gr_ref, rx_ref, rg_ref, h_scr):
    j = pl.program_id(1)

    @pl.when(j == 0)
    def _():
        h = _modulated_norm(x_ref[...], nw_ref[...], sc_ref[...], sh_ref[...])
        h_scr[...] = h.astype(BF16)

    def proj(w_ref):
        return jnp.dot(h_scr[...], w_ref[...], preferred_element_type=F32)

    @pl.when(j == 0)
    def _():
        _rope_store(q_ref, proj(wa_ref), cos_ref[...], sin_ref[...], HEAD_DIM ** -0.5)

    @pl.when(j == 1)
    def _():
        _rope_store(k_ref, proj(wa_ref), cos_ref[...], sin_ref[...], 1.0)

    for step, ref in ((2, v_ref), (3, ga_ref), (4, gr_ref)):
        @pl.when(j == step)
        def _(ref=ref):
            ref[...] = proj(wa_ref)

    for step, ref in ((5, rx_ref), (6, rg_ref)):
        @pl.when(j == step)
        def _(ref=ref):
            ref[...] = proj(wb_ref)


def _inproj(x, ada, nw, cos, sin, wa, wb, tiles_per_group):
    n = x.shape[0]
    tm = min(ROW_TILE, n)
    n_pos_tiles = cos.shape[0] // tm
    n_a, n_b = wa.shape[0], wb.shape[0]
    row = pl.BlockSpec((tm, D_MODEL), lambda i, j: (i, 0))
    row_r = pl.BlockSpec((tm, D_RNN), lambda i, j: (i, 0))
    tab = pl.BlockSpec((tm, 128), lambda i, j: (i % n_pos_tiles, 0))
    return pl.pallas_call(
        _inproj_kernel,
        grid=(n // tm, n_a + n_b),
        in_specs=[
            row,
            _mod_spec(ada, 3, tiles_per_group),
            _mod_spec(ada, 4, tiles_per_group),
            pl.BlockSpec((1, D_MODEL), lambda i, j: (0, 0)),
            tab, tab,
            pl.BlockSpec((None, D_MODEL, D_MODEL), lambda i, j: (jnp.minimum(j, n_a - 1), 0, 0)),
            pl.BlockSpec((None, D_MODEL, D_RNN), lambda i, j: (jnp.maximum(j - n_a, 0), 0, 0)),
        ],
        out_specs=[row, row, row, row, row, row_r, row_r],
        out_shape=[jax.ShapeDtypeStruct((n, D_MODEL), BF16)]
        + [jax.ShapeDtypeStruct((n, D_MODEL), F32)] * 4
        + [jax.ShapeDtypeStruct((n, D_RNN), F32)] * 2,
        scratch_shapes=[pltpu.VMEM((tm, D_MODEL), BF16)],
        compiler_params=_params(("arbitrary", "arbitrary")),
        name="inproj",
    )(x, ada, ada, nw.reshape(1, D_MODEL), cos, sin, wa, wb)


def _subln(o1, l1, o2, l2, lam, w, li):
    o = o1 / l1 - lam * (o2 / l2)
    return (_rms(o, SUBLN_EPS) * w) * (1.0 - li)


def _flash_kernel(q_ref, k_ref, v_ref, lam_ref, w_ref, o_ref, kt_scr, vb_scr, *, li):
    seq = q_ref.shape[0]
    tq = ATT_TILE
    kt_scr[...] = k_ref[...].T.astype(BF16)
    vb_scr[...] = v_ref[...].astype(BF16)
    lam = _lam(lam_ref, li)
    w = w_ref[...]
    lane = lax.broadcasted_iota(jnp.int32, (tq, 2 * HEAD_DIM), 1)
    causal = (lax.broadcasted_iota(jnp.int32, (tq, tq), 1) <= lax.broadcasted_iota(jnp.int32, (tq, tq), 0))
    for i in range(seq // tq):
        lo = i * tq
        q = q_ref[lo:lo + tq, :]
        outs = []
        for sel in (lane < HEAD_DIM, lane >= HEAD_DIM):
            qm = jnp.where(sel, q, jnp.zeros_like(q))
            sd = jnp.dot(qm, kt_scr[:, lo:lo + tq], preferred_element_type=F32)
            sd = jnp.where(causal, sd, NEG_INF)
            m = jnp.max(sd, axis=-1, keepdims=True)
            if i:
                so = jnp.dot(qm, kt_scr[:, :lo], preferred_element_type=F32)
                m = jnp.maximum(m, jnp.max(so, axis=-1, keepdims=True))
            ed = jnp.exp(sd - m)
            l = jnp.sum(ed, axis=-1, keepdims=True)
            acc = jnp.dot(ed.astype(BF16), vb_scr[lo:lo + tq, :], preferred_element_type=F32)
            if i:
                eo = jnp.exp(so - m)
                l = l + jnp.sum(eo, axis=-1, keepdims=True)
                acc = acc + jnp.dot(eo.astype(BF16), vb_scr[:lo, :], preferred_element_type=F32)
            outs.append(acc / l)
        o_ref[lo:lo + tq, :] = (_rms(outs[0] - lam * outs[1], SUBLN_EPS) * w) * (1.0 - li)


def _flash(q, k, v, lam_rows, subln_w, batch, seq, li):
    spec = pl.BlockSpec((seq, 2 * HEAD_DIM), lambda b, h: (b, h))
    return pl.pallas_call(
        functools.partial(_flash_kernel, li=li),
        grid=(batch, N_HEADS),
        in_specs=[spec, spec, spec,
                  pl.BlockSpec((4, HEAD_DIM), lambda b, h: (0, 0)),
                  pl.BlockSpec((1, V_DIM), lambda b, h: (0, 0))],
        out_specs=spec,
        out_shape=jax.ShapeDtypeStruct((batch * seq, D_MODEL), F32),
        scratch_shapes=[pltpu.VMEM((2 * HEAD_DIM, seq), BF16), pltpu.VMEM((seq, 2 * HEAD_DIM), BF16)],
        compiler_params=_params(("arbitrary", "arbitrary")),
        name="flash",
    )(q, k, v, lam_rows, subln_w.reshape(1, V_DIM))


def _paged_kernel(pt_ref, qbd_ref, kn_ref, vn_ref, lam_ref, w_ref, spread_ref, *rest, li):
    npg = PAGES_PER_STEP
    k_refs, v_refs = rest[:npg], rest[npg:2 * npg]
    o_ref, m_scr, l_scr, a_scr = rest[2 * npg:]
    hf = pl.program_id(1)
    qbd = qbd_ref[...]
    n_new = kn_ref.shape[0]
    n_rows = qbd.shape[0]
    half = n_rows // 2
    width = PAGE_SIZE * N_HEADS

    @pl.when(hf == 0)
    def _():
        nt = (((1,), (1,)), ((), ()))
        s = lax.dot_general(qbd, kn_ref[...].astype(BF16), nt, preferred_element_type=F32)
        row = lax.broadcasted_iota(jnp.int32, s.shape, 0)
        col = lax.broadcasted_iota(jnp.int32, s.shape, 1)
        s = jnp.where(col <= (row % half) // N_HEADS, s, NEG_INF)
        m = jnp.max(s, axis=-1, keepdims=True)
        e = jnp.exp(s - m)
        acc = jnp.zeros(a_scr.shape, F32)
        for t in range(n_new):
            acc = acc + e[:, t:t + 1] * jnp.tile(vn_ref[t], (n_rows // N_HEADS, 1))
        m_scr[...] = jnp.broadcast_to(m, m_scr.shape)
        l_scr[...] = jnp.broadcast_to(jnp.sum(e, axis=-1, keepdims=True), l_scr.shape)
        a_scr[...] = acc

    kt = jnp.concatenate([k_refs[p][...].reshape(N_QK * HEAD_DIM, PAGE_SIZE).astype(BF16) for p in range(npg)],
                         axis=1)
    s = jnp.dot(qbd, kt, preferred_element_type=F32)
    m_old = m_scr[:, 0:1]
    m_new = jnp.maximum(m_old, jnp.max(s, axis=-1, keepdims=True))
    alpha = jnp.exp(m_old - m_new)
    e = jnp.exp(s - m_new)
    l_scr[...] = jnp.broadcast_to(alpha * l_scr[:, 0:1] + jnp.sum(e, axis=-1, keepdims=True), l_scr.shape)
    e = e.astype(BF16)
    row = lax.broadcasted_iota(jnp.int32, (n_rows, width), 0)
    col = lax.broadcasted_iota(jnp.int32, (n_rows, width), 1)
    own_head = (row % N_HEADS) == (col % N_HEADS)
    acc = alpha * a_scr[...]
    for p in range(npg):
        ep = jnp.dot(e[:, p * PAGE_SIZE:(p + 1) * PAGE_SIZE], spread_ref[...], preferred_element_type=F32)
        ep = jnp.where(own_head, ep, 0.0).astype(BF16)
        vp = v_refs[p][...].reshape(width, V_DIM).astype(BF16)
        acc = acc + jnp.dot(ep, vp, preferred_element_type=F32)
    a_scr[...] = acc
    m_scr[...] = jnp.broadcast_to(m_new, m_scr.shape)

    @pl.when(hf == pl.num_programs(1) - 1)
    def _():
        a = a_scr[...]
        l = l_scr[:, 0:1]
        o_ref[...] = _subln(a[:half], l[:half], a[half:], l[half:], _lam(lam_ref, li), w_ref[...], li)


def _paged(page_table, qbd, kn, vn, lam_rows, subln_w, cache_kt, cache_v, layer, li):
    batch, n_pages = page_table.shape
    npg = PAGES_PER_STEP
    steps = n_pages // npg
    n_new = kn.shape[1]
    n_rows = qbd.shape[1]

    def page_spec(shape, p):
        return pl.BlockSpec((None, None) + shape,
                            lambda b, hf, pt: (layer, pt[b * n_pages + hf * npg + p], 0, 0, 0))

    spread = jnp.repeat(jnp.eye(PAGE_SIZE, dtype=BF16), N_HEADS, axis=1)
    stat = pltpu.VMEM((n_rows, 128), F32)
    return pl.pallas_call(
        functools.partial(_paged_kernel, li=li),
        grid_spec=pltpu.PrefetchScalarGridSpec(
            num_scalar_prefetch=1,
            grid=(batch, steps),
            in_specs=[
                pl.BlockSpec((None, n_rows, D_MODEL), lambda b, hf, pt: (b, 0, 0)),
                pl.BlockSpec((None, n_new, D_MODEL), lambda b, hf, pt: (b, 0, 0)),
                pl.BlockSpec((None, n_new, N_HEADS, V_DIM), lambda b, hf, pt: (b, 0, 0, 0)),
                pl.BlockSpec((4, HEAD_DIM), lambda b, hf, pt: (0, 0)),
                pl.BlockSpec((1, V_DIM), lambda b, hf, pt: (0, 0)),
                pl.BlockSpec((PAGE_SIZE, PAGE_SIZE * N_HEADS), lambda b, hf, pt: (0, 0)),
            ]
            + [page_spec((N_QK, HEAD_DIM, PAGE_SIZE), p) for p in range(npg)]
            + [page_spec((PAGE_SIZE, N_HEADS, V_DIM), p) for p in range(npg)],
            out_specs=pl.BlockSpec((None, n_rows // 2, V_DIM), lambda b, hf, pt: (b, 0, 0)),
            scratch_shapes=[stat, stat, stat],
        ),
        out_shape=jax.ShapeDtypeStruct((batch, n_rows // 2, V_DIM), F32),
        compiler_params=_params(("arbitrary", "arbitrary")),
        name="paged",
    )(page_table.reshape(-1), qbd, kn, vn, lam_rows, subln_w.reshape(1, V_DIM), spread,
      *([cache_kt] * npg), *([cache_v] * npg))


def _log_sigmoid(x):
    return -(jnp.maximum(-x, 0.0) + jnp.log1p(jnp.exp(-jnp.abs(x))))


def _gelu_tanh(x):
    return 0.5 * x * (1.0 + jnp.tanh(math.sqrt(2.0 / math.pi) * (x + 0.044715 * (x * x * x))))


def _lru_coeffs(xc, wa_ref, ba, wx_ref, bx, lru):
    ra, ri = [], []
    for n in range(N_RNN_BLOCKS):
        xb = xc[:, n * RNN_BW:(n + 1) * RNN_BW].astype(BF16)
        ra.append(jnp.dot(xb, wa_ref[n], preferred_element_type=F32))
        ri.append(jnp.dot(xb, wx_ref[n], preferred_element_type=F32))
    r = jax.nn.sigmoid(jnp.concatenate(ra, axis=1) + ba)
    i = jax.nn.sigmoid(jnp.concatenate(ri, axis=1) + bx)
    log_a = LRU_C * r * _log_sigmoid(lru)
    a = jnp.exp(log_a)
    b = jnp.sqrt(-jnp.tanh(log_a) * (a * a + 1.0)) * (i * xc)
    return a, b


def _rnn_kernel(rx_ref, rg_ref, cw_ref, cb_ref, wa_ref, ba_ref, wx_ref, bx_ref, lru_ref,
                hg_ref, hl_ref, xext, a_scr, b_scr, h_scr, carry):
    t = pl.program_id(1)
    tc = rx_ref.shape[0]

    @pl.when(t == 0)
    def _():
        xext[0:8, :] = jnp.zeros((8, D_RNN), F32)
        carry[...] = jnp.zeros_like(carry)

    x = rx_ref[...]
    xext[8:8 + tc, :] = x
    xc = cb_ref[...] + cw_ref[3:4, :] * x
    for j in range(CONV_W - 1):
        xc = xc + cw_ref[j:j + 1, :] * xext[5 + j:5 + j + tc, :]
    xext[0:8, :] = x[tc - 8:tc, :]

    a, b = _lru_coeffs(xc, wa_ref, ba_ref[...], wx_ref, bx_ref[...], lru_ref[...])
    sub = lax.broadcasted_iota(jnp.int32, a.shape, 0) % 8
    for d in (1, 2, 4):
        keep = sub >= d
        a_sh = jnp.where(keep, pltpu.roll(a, d, 0), 1.0)
        b_sh = jnp.where(keep, pltpu.roll(b, d, 0), 0.0)
        b = a * b_sh + b
        a = a * a_sh
    a_scr[...] = a
    b_scr[...] = b

    def group(g, c):
        r = pl.multiple_of(g * 8, 8)
        h = a_scr[pl.ds(r, 8), :] * c + b_scr[pl.ds(r, 8), :]
        h_scr[pl.ds(r, 8), :] = h
        return jnp.broadcast_to(h[7:8, :], (8, D_RNN))

    c = lax.fori_loop(0, tc // 8, group, carry[...])
    carry[...] = c
    hg_ref[...] = (h_scr[...] * _gelu_tanh(rg_ref[...])).astype(BF16)

    @pl.when(t == pl.num_programs(1) - 1)
    def _():
        hl_ref[...] = c[0:1, :]


def _rnn_weight_specs():
    def full(shape):
        return pl.BlockSpec(shape, lambda *_: (0,) * len(shape))

    return [full((CONV_W, D_RNN)), full((1, D_RNN)),
            full((N_RNN_BLOCKS, RNN_BW, RNN_BW)), full((1, D_RNN)),
            full((N_RNN_BLOCKS, RNN_BW, RNN_BW)), full((1, D_RNN)), full((1, D_RNN))]


def _rnn(rx, rg, rnn_w, batch, seq):
    tc = RNN_TILE
    nt = seq // tc
    row = pl.BlockSpec((tc, D_RNN), lambda b, t: (b * nt + t, 0))
    big = pltpu.VMEM((tc, D_RNN), F32)
    return pl.pallas_call(
        _rnn_kernel,
        grid=(batch, nt),
        in_specs=[row, row] + _rnn_weight_specs(),
        out_specs=[row, pl.BlockSpec((None, 1, D_RNN), lambda b, t: (b, 0, 0))],
        out_shape=[jax.ShapeDtypeStruct((batch * seq, D_RNN), BF16),
                   jax.ShapeDtypeStruct((batch, 1, D_RNN), F32)],
        scratch_shapes=[pltpu.VMEM((tc + 8, D_RNN), F32), big, big, big, pltpu.VMEM((8, D_RNN), F32)],
        compiler_params=_params(("arbitrary", "arbitrary")),
        name="rnn",
    )(rx, rg, *rnn_w)


def _rnn_step_kernel(rx_ref, rg_ref, cs_ref, h0_ref, cw_ref, cb_ref, wa_ref, ba_ref, wx_ref, bx_ref, lru_ref,
                     hg_ref, hl_ref):
    steps = rx_ref.shape[0]
    xs = [cs_ref[j] for j in range(CONV_W - 1)] + [rx_ref[t] for t in range(steps)]
    h = h0_ref[...]
    for t in range(steps):
        xc = cb_ref[...] + sum(cw_ref[j:j + 1, :] * xs[t + j] for j in range(CONV_W))
        a, b = _lru_coeffs(xc, wa_ref, ba_ref[...], wx_ref, bx_ref[...], lru_ref[...])
        h = a * h + b
        hg_ref[t] = (h * _gelu_tanh(rg_ref[t])).astype(BF16)
    hl_ref[...] = h


def _rnn_step(rx_t, rg_t, cs_t, h0, rnn_w):
    steps, batch, _ = rx_t.shape

    def full(shape):
        return pl.BlockSpec(shape, lambda i: (0,) * len(shape))

    return pl.pallas_call(
        _rnn_step_kernel,
        grid=(1,),
        in_specs=[full(rx_t.shape), full(rg_t.shape), full(cs_t.shape), full(h0.shape)] + _rnn_weight_specs(),
        out_specs=[full(rx_t.shape), full(h0.shape)],
        out_shape=[jax.ShapeDtypeStruct(rx_t.shape, BF16), jax.ShapeDtypeStruct(h0.shape, F32)],
        compiler_params=_params(("arbitrary",)),
        name="rnn_step",
    )(rx_t, rg_t, cs_t, h0, *rnn_w)


def _merge_kernel(x_ref, o_ref, hg_ref, ga_ref, gr_ref, gt_ref, watt_ref, wrnn_ref, wo_ref, y_ref):
    ya = jnp.dot(o_ref[...].astype(BF16), watt_ref[...], preferred_element_type=F32)
    yr = jnp.dot(hg_ref[...], wrnn_ref[...], preferred_element_type=F32)
    merged = jax.nn.sigmoid(ga_ref[...]) * ya + jax.nn.sigmoid(gr_ref[...]) * yr
    y_ref[...] = x_ref[...] + gt_ref[...] * jnp.dot(merged.astype(BF16), wo_ref[...],
                                                     preferred_element_type=F32)


def _merge(x, o, hg, ga, gr, ada, w_att, w_rnn, w_o, tiles_per_group):
    n = x.shape[0]
    tm = min(ROW_TILE, n)
    row = pl.BlockSpec((tm, D_MODEL), lambda i: (i, 0))

    def full(shape):
        return pl.BlockSpec(shape, lambda i: (0, 0))

    return pl.pallas_call(
        _merge_kernel,
        grid=(n // tm,),
        in_specs=[row, row, pl.BlockSpec((tm, D_RNN), lambda i: (i, 0)), row, row,
                  _mod_spec(ada, 5, tiles_per_group),
                  full((D_MODEL, D_MODEL)), full((D_RNN, D_MODEL)), full((D_MODEL, D_MODEL))],
        out_specs=row,
        out_shape=jax.ShapeDtypeStruct((n, D_MODEL), F32),
        compiler_params=_params(("arbitrary",)),
        name="merge",
    )(x, o, hg, ga, gr, ada, w_att, w_rnn, w_o)


def _rope_tables(pos):
    half = HEAD_DIM // 2
    inv = 1.0 / (ROPE_THETA ** (jnp.arange(half, dtype=F32) * 2.0 / HEAD_DIM))
    ang = pos.astype(F32)[:, None] * inv[None, :]
    cos, sin = jnp.cos(ang), jnp.sin(ang)
    return jnp.tile(jnp.concatenate([cos, cos], axis=1), (1, 2)), jnp.tile(jnp.concatenate([-sin, sin], axis=1), (1, 2))


def kernel(x_prompt, x_sample, cache_k, cache_v, state_conv, state_h, page_table, c_prompt, c_sample, norm_w, w_ada, b_ada, ffn1_wi, ffn1_wo, ffn2_wi, ffn2_wo, w_in, lam_q1, lam_k1, lam_q2, lam_k2, subln_w, conv_w, conv_b, w_rg_a, b_rg_a, w_rg_x, b_rg_x, lru_param, w_att_br, w_rnn_br, w_o, norm_f):
    bp, tp, _ = x_prompt.shape
    bs, ts, _ = x_sample.shape
    past_len = page_table.shape[1] * PAGE_SIZE
    assert tp % ATT_TILE == 0 and tp % ROW_TILE == 0 and tp % RNN_TILE == 0
    assert page_table.shape[1] % PAGES_PER_STEP == 0 and ts == 4

    ada = _ada(jnp.concatenate([c_prompt, c_sample], axis=0), w_ada, b_ada)
    ada_p = ada[:, :bp, None, :]
    ada_s = jnp.repeat(ada[:, bp:], ts, axis=1)[:, None]
    tiles_per_seq = tp // ROW_TILE

    cos_p, sin_p = _rope_tables(jnp.arange(tp))
    cos_s, sin_s = _rope_tables(jnp.tile(past_len + jnp.arange(ts), bs))

    cache_kt = jnp.transpose(cache_k, (0, 1, 3, 4, 2))

    xp = x_prompt.reshape(bp * tp, D_MODEL)
    xs = x_sample.reshape(bs * ts, D_MODEL)
    outs_p = [[] for _ in range(4)]
    outs_s = [[] for _ in range(4)]
    s_q, s_k, s_v = D_MODEL, 2 * D_MODEL, 3 * D_MODEL
    s_rx, s_rg, s_ga = s_v + D_RNN, s_v + 2 * D_RNN, s_v + 2 * D_RNN + D_MODEL

    for l in range(DEPTH):
        li = _lambda_init(l)
        bf = lambda w: w.astype(BF16)
        wi1, wo1, wi2, wo2 = bf(ffn1_wi[l]), bf(ffn1_wo[l]), bf(ffn2_wi[l]), bf(ffn2_wo[l])
        wl = w_in[l]
        wa = bf(jnp.stack([wl[:, :s_q], wl[:, s_q:s_k], wl[:, s_k:s_v], wl[:, s_rg:s_ga], wl[:, s_ga:]]))
        wb = bf(jnp.stack([wl[:, s_v:s_rx], wl[:, s_rx:s_rg]]))
        w_att, w_rnn, w_out = bf(w_att_br[l]), bf(w_rnn_br[l]), bf(w_o[l])
        rnn_w = (conv_w[l], conv_b[l][None], bf(w_rg_a[l]), b_rg_a[l][None], bf(w_rg_x[l]), b_rg_x[l][None],
                 lru_param[l][None])
        lam_rows = jnp.stack([lam_q1[l], lam_k1[l], lam_q2[l], lam_k2[l]])
        last = l == DEPTH - 1

        xp = _ffn(xp, ada_p[l], 0, norm_w[l, 0], wi1, wo1, tiles_per_seq)
        q, k, v, ga, gr, rx, rg = _inproj(xp, ada_p[l], norm_w[l, 1], cos_p, sin_p, wa, wb, tiles_per_seq)
        o = _flash(q, k, v, lam_rows, subln_w[l], bp, tp, li)
        hg, h_last = _rnn(rx, rg, rnn_w, bp, tp)
        xp = _merge(xp, o, hg, ga, gr, ada_p[l], w_att, w_rnn, w_out, tiles_per_seq)
        xp = _ffn(xp, ada_p[l], 2, norm_w[l, 2], wi2, wo2, tiles_per_seq, norm_f if last else None)
        outs_p[0].append(k.reshape(bp, tp, N_QK, HEAD_DIM))
        outs_p[1].append(v.reshape(bp, tp, N_HEADS, V_DIM))
        outs_p[2].append(rx.reshape(bp, tp, D_RNN)[:, tp - (CONV_W - 1):])
        outs_p[3].append(h_last.reshape(bp, D_RNN))

        xs = _ffn(xs, ada_s[l], 0, norm_w[l, 0], wi1, wo1, 1)
        q, k, v, ga, gr, rx, rg = _inproj(xs, ada_s[l], norm_w[l, 1], cos_s, sin_s, wa, wb, 1)
        qe = q.reshape(bs, ts, N_HEADS, 2, HEAD_DIM)
        qbd = jnp.einsum('bqhjd,hH,jJ->bjqhHJd', qe, jnp.eye(N_HEADS, dtype=BF16), jnp.eye(2, dtype=BF16))
        qbd = qbd.reshape(bs, 2 * ts * N_HEADS, D_MODEL)
        o = _paged(page_table, qbd, k.reshape(bs, ts, D_MODEL), v.reshape(bs, ts, N_HEADS, V_DIM),
                   lam_rows, subln_w[l], cache_kt, cache_v, l, li)
        o = o.reshape(bs * ts, D_MODEL)
        tm = lambda a: a.reshape(bs, ts, D_RNN).transpose(1, 0, 2)
        cs_t = jnp.concatenate([state_conv[l].transpose(1, 0, 2), tm(rx)], axis=0)
        hg_t, h_last = _rnn_step(tm(rx), tm(rg), cs_t[:CONV_W - 1], state_h[l], rnn_w)
        hg = hg_t.transpose(1, 0, 2).reshape(bs * ts, D_RNN)
        xs = _merge(xs, o, hg, ga, gr, ada_s[l], w_att, w_rnn, w_out, 1)
        xs = _ffn(xs, ada_s[l], 2, norm_w[l, 2], wi2, wo2, 1, norm_f if last else None)
        outs_s[0].append(k.reshape(bs, ts, N_QK, HEAD_DIM))
        outs_s[1].append(v.reshape(bs, ts, N_HEADS, V_DIM))
        outs_s[2].append(cs_t[ts:].transpose(1, 0, 2))
        outs_s[3].append(h_last)

    stack = lambda xs_: jnp.stack(xs_)
    return (xp.reshape(bp, tp, D_MODEL), xs.reshape(bs, ts, D_MODEL),
            stack(outs_p[0]), stack(outs_p[1]), stack(outs_p[2]), stack(outs_p[3]),
            stack(outs_s[0]), stack(outs_s[1]), stack(outs_s[2]), stack(outs_s[3]))
```

```python
import functools
import math

import jax
import jax.numpy as jnp
from jax import lax
from jax.experimental import pallas as pl
from jax.experimental.pallas import tpu as pltpu

F32 = jnp.float32
BF16 = jnp.bfloat16

D_MODEL = 1024
DEPTH = 4
N_HEADS = 8
HEAD_DIM = 64
N_QK = 2 * N_HEADS
V_DIM = 2 * HEAD_DIM
D_RNN = 1280
N_RNN_BLOCKS = 10
RNN_BW = D_RNN // N_RNN_BLOCKS
CONV_W = 4
LRU_C = 8.0
D_FF = 2816
ROPE_THETA = 10000.0
PAGE_SIZE = 128
NORM_EPS = 1e-6
SUBLN_EPS = 1e-5
NEG_INF = -1e30

V7X_VMEM_BYTES = 64 * 1024 * 1024
VMEM_LIMIT = V7X_VMEM_BYTES - 8 * 1024 * 1024

ROW_TILE = 512
FF_TILE = 1408
ATT_TILE = 256
RNN_TILE = 256
PAGES_PER_STEP = 16


def _lambda_init(l):
    return 0.8 - 0.6 * math.exp(-0.3 * l)


def _params(sem, vmem=VMEM_LIMIT):
    return pltpu.CompilerParams(dimension_semantics=sem, vmem_limit_bytes=vmem)


def _rms(x, eps):
    return x * lax.rsqrt(jnp.mean(x * x, axis=-1, keepdims=True) + eps)


def _modulated_norm(x, nw, sc, sh):
    return (_rms(x, NORM_EPS) * nw) * (1.0 + sc) + sh


def _lam(lam_ref, li):
    a = jnp.sum(lam_ref[0:1, :] * lam_ref[1:2, :], axis=-1, keepdims=True)
    b = jnp.sum(lam_ref[2:3, :] * lam_ref[3:4, :], axis=-1, keepdims=True)
    return jnp.exp(a) - jnp.exp(b) + li


def _ada_kernel(c_ref, w_ref, b_ref, o_ref):
    c = c_ref[...]
    s = (c * jax.nn.sigmoid(c)).astype(BF16)
    o_ref[...] = jnp.dot(s, w_ref[...].astype(BF16), preferred_element_type=F32) + b_ref[...]


def _ada(c_all, w_ada, b_ada):
    m = c_all.shape[0]
    n_col = w_ada.shape[2] // D_MODEL
    return pl.pallas_call(
        _ada_kernel,
        grid=(DEPTH, n_col),
        in_specs=[
            pl.BlockSpec((m, D_MODEL), lambda l, j: (0, 0)),
            pl.BlockSpec((None, D_MODEL, D_MODEL), lambda l, j: (l, 0, j)),
            pl.BlockSpec((None, 1, D_MODEL), lambda l, j: (l, 0, j)),
        ],
        out_specs=pl.BlockSpec((None, m, D_MODEL), lambda l, j: (l, 0, j)),
        out_shape=jax.ShapeDtypeStruct((DEPTH, m, w_ada.shape[2]), F32),
        compiler_params=_params(("arbitrary", "arbitrary")),
        name="ada",
    )(c_all, w_ada, b_ada.reshape(DEPTH, 1, -1))


def _mod_spec(ada, col, tiles_per_group):
    _, r, _ = ada.shape
    return pl.BlockSpec((None, r, D_MODEL), lambda i, *_: (i // tiles_per_group, 0, col))


def _ffn_kernel(x_ref, sh_ref, sc_ref, gt_ref, nw_ref, wg_ref, wu_ref, wo_ref, *rest, final):
    if final:
        nf_ref, o_ref, h_scr, acc_scr = rest
    else:
        o_ref, h_scr, acc_scr = rest
    j = pl.program_id(1)

    @pl.when(j == 0)
    def _():
        h = _modulated_norm(x_ref[...], nw_ref[...], sc_ref[...], sh_ref[...])
        h_scr[...] = h.astype(BF16)
        acc_scr[...] = jnp.zeros_like(acc_scr)

    h = h_scr[...]
    g = jnp.dot(h, wg_ref[...], preferred_element_type=F32)
    u = jnp.dot(h, wu_ref[...], preferred_element_type=F32)
    a = (g * jax.nn.sigmoid(g) * u).astype(BF16)
    acc_scr[...] += jnp.dot(a, wo_ref[...], preferred_element_type=F32)

    @pl.when(j == pl.num_programs(1) - 1)
    def _():
        y = x_ref[...] + (0.5 * gt_ref[...]) * acc_scr[...]
        if final:
            y = _rms(y, NORM_EPS) * nf_ref[...]
        o_ref[...] = y


def _ffn(x, ada, sub, nw, wi, wo, tiles_per_group, norm_f=None):
    n = x.shape[0]
    tm = min(ROW_TILE, n)
    n_f = D_FF // FF_TILE
    final = norm_f is not None
    row = pl.BlockSpec((tm, D_MODEL), lambda i, j: (i, 0))
    vec = pl.BlockSpec((1, D_MODEL), lambda i, j: (0, 0))
    in_specs = [
        row,
        _mod_spec(ada, 3 * sub + 0, tiles_per_group),
        _mod_spec(ada, 3 * sub + 1, tiles_per_group),
        _mod_spec(ada, 3 * sub + 2, tiles_per_group),
        vec,
        pl.BlockSpec((D_MODEL, FF_TILE), lambda i, j: (0, j)),
        pl.BlockSpec((D_MODEL, FF_TILE), lambda i, j: (0, j + n_f)),
        pl.BlockSpec((FF_TILE, D_MODEL), lambda i, j: (j, 0)),
    ]
    args = [x, ada, ada, ada, nw.reshape(1, D_MODEL), wi, wi, wo]
    if final:
        in_specs.append(vec)
        args.append(norm_f.reshape(1, D_MODEL))
    return pl.pallas_call(
        functools.partial(_ffn_kernel, final=final),
        grid=(n // tm, n_f),
        in_specs=in_specs,
        out_specs=row,
        out_shape=jax.ShapeDtypeStruct((n, D_MODEL), F32),
        scratch_shapes=[pltpu.VMEM((tm, D_MODEL), BF16), pltpu.VMEM((tm, D_MODEL), F32)],
        compiler_params=_params(("arbitrary", "arbitrary")),
        name="ffn",
    )(*args)


def _rope_store(o_ref, r, cos, sin, scale):
    lane = lax.broadcasted_iota(jnp.int32, cos.shape, 1)
    first = (lane % HEAD_DIM) < (HEAD_DIM // 2)
    for c in range(D_MODEL // 128):
        blk = r[:, c * 128:(c + 1) * 128]
        swapped = jnp.where(first, pltpu.roll(blk, 128 - HEAD_DIM // 2, 1),
                            pltpu.roll(blk, HEAD_DIM // 2, 1))
        out = blk * cos + swapped * sin
        if scale != 1.0:
            out = out * scale
        o_ref[:, c * 128:(c + 1) * 128] = out.astype(o_ref.dtype)


def _inproj_kernel(x_ref, sh_ref, sc_ref, nw_ref, cos_ref, sin_ref, wa_ref, wb_ref,
                   q_ref, k_ref, v_ref, ga_ref, gr_ref, rx_ref, rg_ref, h_scr):
    j = pl.program_id(1)

    @pl.when(j == 0)
    def _():
        h = _modulated_norm(x_ref[...], nw_ref[...], sc_ref[...], sh_ref[...])
        h_scr[...] = h.astype(BF16)

    def proj(w_ref):
        return jnp.dot(h_scr[...], w_ref[...], preferred_element_type=F32)

    @pl.when(j == 0)
    def _():
        _rope_store(q_ref, proj(wa_ref), cos_ref[...], sin_ref[...], HEAD_DIM ** -0.5)

    @pl.when(j == 1)
    def _():
        _rope_store(k_ref, proj(wa_ref), cos_ref[...], sin_ref[...], 1.0)

    for step, ref in ((2, v_ref), (3, ga_ref), (4, gr_ref)):
        @pl.when(j == step)
        def _(ref=ref):
            ref[...] = proj(wa_ref)

    for step, ref in ((5, rx_ref), (6, rg_ref)):
        @pl.when(j == step)
        def _(ref=ref):
            ref[...] = proj(wb_ref)


def _inproj(x, ada, nw, cos, sin, wa, wb, tiles_per_group):
    n = x.shape[0]
    tm = min(ROW_TILE, n)
    n_pos_tiles = cos.shape[0] // tm
    n_a, n_b = wa.shape[0], wb.shape[0]
    row = pl.BlockSpec((tm, D_MODEL), lambda i, j: (i, 0))
    row_r = pl.BlockSpec((tm, D_RNN), lambda i, j: (i, 0))
    tab = pl.BlockSpec((tm, 128), lambda i, j: (i % n_pos_tiles, 0))
    return pl.pallas_call(
        _inproj_kernel,
        grid=(n // tm, n_a + n_b),
        in_specs=[
            row,
            _mod_spec(ada, 3, tiles_per_group),
            _mod_spec(ada, 4, tiles_per_group),
            pl.BlockSpec((1, D_MODEL), lambda i, j: (0, 0)),
            tab, tab,
            pl.BlockSpec((None, D_MODEL, D_MODEL), lambda i, j: (jnp.minimum(j, n_a - 1), 0, 0)),
            pl.BlockSpec((None, D_MODEL, D_RNN), lambda i, j: (jnp.maximum(j - n_a, 0), 0, 0)),
        ],
        out_specs=[row, row, row, row, row, row_r, row_r],
        out_shape=[jax.ShapeDtypeStruct((n, D_MODEL), BF16)]
        + [jax.ShapeDtypeStruct((n, D_MODEL), F32)] * 4
        + [jax.ShapeDtypeStruct((n, D_RNN), F32)] * 2,
        scratch_shapes=[pltpu.VMEM((tm, D_MODEL), BF16)],
        compiler_params=_params(("arbitrary", "arbitrary")),
        name="inproj",
    )(x, ada, ada, nw.reshape(1, D_MODEL), cos, sin, wa, wb)


def _subln(o1, l1, o2, l2, lam, w, li):
    o = o1 / l1 - lam * (o2 / l2)
    return (_rms(o, SUBLN_EPS) * w) * (1.0 - li)


def _flash_kernel(q_ref, k_ref, v_ref, lam_ref, w_ref, o_ref, kt_scr, vb_scr, *, li):
    seq = q_ref.shape[0]
    tq = ATT_TILE
    kt_scr[...] = k_ref[...].T.astype(BF16)
    vb_scr[...] = v_ref[...].astype(BF16)
    lam = _lam(lam_ref, li)
    w = w_ref[...]
    lane = lax.broadcasted_iota(jnp.int32, (tq, 2 * HEAD_DIM), 1)
    causal = (lax.broadcasted_iota(jnp.int32, (tq, tq), 1) <= lax.broadcasted_iota(jnp.int32, (tq, tq), 0))
    for i in range(seq // tq):
        lo = i * tq
        q = q_ref[lo:lo + tq, :]
        outs = []
        for sel in (lane < HEAD_DIM, lane >= HEAD_DIM):
            qm = jnp.where(sel, q, jnp.zeros_like(q))
            sd = jnp.dot(qm, kt_scr[:, lo:lo + tq], preferred_element_type=F32)
            sd = jnp.where(causal, sd, NEG_INF)
            m = jnp.max(sd, axis=-1, keepdims=True)
            if i:
                so = jnp.dot(qm, kt_scr[:, :lo], preferred_element_type=F32)
                m = jnp.maximum(m, jnp.max(so, axis=-1, keepdims=True))
            ed = jnp.exp(sd - m)
            l = jnp.sum(ed, axis=-1, keepdims=True)
            acc = jnp.dot(ed.astype(BF16), vb_scr[lo:lo + tq, :], preferred_element_type=F32)
            if i:
                eo = jnp.exp(so - m)
                l = l + jnp.sum(eo, axis=-1, keepdims=True)
                acc = acc + jnp.dot(eo.astype(BF16), vb_scr[:lo, :], preferred_element_type=F32)
            outs.append(acc / l)
        o_ref[lo:lo + tq, :] = (_rms(outs[0] - lam * outs[1], SUBLN_EPS) * w) * (1.0 - li)


def _flash(q, k, v, lam_rows, subln_w, batch, seq, li):
    spec = pl.BlockSpec((seq, 2 * HEAD_DIM), lambda b, h: (b, h))
    return pl.pallas_call(
        functools.partial(_flash_kernel, li=li),
        grid=(batch, N_HEADS),
        in_specs=[spec, spec, spec,
                  pl.BlockSpec((4, HEAD_DIM), lambda b, h: (0, 0)),
                  pl.BlockSpec((1, V_DIM), lambda b, h: (0, 0))],
        out_specs=spec,
        out_shape=jax.ShapeDtypeStruct((batch * seq, D_MODEL), F32),
        scratch_shapes=[pltpu.VMEM((2 * HEAD_DIM, seq), BF16), pltpu.VMEM((seq, 2 * HEAD_DIM), BF16)],
        compiler_params=_params(("arbitrary", "arbitrary")),
        name="flash",
    )(q, k, v, lam_rows, subln_w.reshape(1, V_DIM))


def _paged_kernel(pt_ref, qbd_ref, kn_ref, vn_ref, lam_ref, w_ref, spread_ref, *rest, li):
    npg = PAGES_PER_STEP
    k_refs, v_refs = rest[:npg], rest[npg:2 * npg]
    o_ref, m_scr, l_scr, a_scr = rest[2 * npg:]
    hf = pl.program_id(1)
    qbd = qbd_ref[...]
    n_new = kn_ref.shape[0]
    n_rows = qbd.shape[0]
    half = n_rows // 2
    width = PAGE_SIZE * N_HEADS

    @pl.when(hf == 0)
    def _():
        nt = (((1,), (1,)), ((), ()))
        s = lax.dot_general(qbd, kn_ref[...].astype(BF16), nt, preferred_element_type=F32)
        row = lax.broadcasted_iota(jnp.int32, s.shape, 0)
        col = lax.broadcasted_iota(jnp.int32, s.shape, 1)
        s = jnp.where(col <= (row % half) // N_HEADS, s, NEG_INF)
        m = jnp.max(s, axis=-1, keepdims=True)
        e = jnp.exp(s - m)
        acc = jnp.zeros(a_scr.shape, F32)
        for t in range(n_new):
            acc = acc + e[:, t:t + 1] * jnp.tile(vn_ref[t], (n_rows // N_HEADS, 1))
        m_scr[...] = jnp.broadcast_to(m, m_scr.shape)
        l_scr[...] = jnp.broadcast_to(jnp.sum(e, axis=-1, keepdims=True), l_scr.shape)
        a_scr[...] = acc

    kt = jnp.concatenate([k_refs[p][...].reshape(N_QK * HEAD_DIM, PAGE_SIZE).astype(BF16) for p in range(npg)],
                         axis=1)
    s = jnp.dot(qbd, kt, preferred_element_type=F32)
    m_old = m_scr[:, 0:1]
    m_new = jnp.maximum(m_old, jnp.max(s, axis=-1, keepdims=True))
    alpha = jnp.exp(m_old - m_new)
    e = jnp.exp(s - m_new)
    l_scr[...] = jnp.broadcast_to(alpha * l_scr[:, 0:1] + jnp.sum(e, axis=-1, keepdims=True), l_scr.shape)
    e = e.astype(BF16)
    row = lax.broadcasted_iota(jnp.int32, (n_rows, width), 0)
    col = lax.broadcasted_iota(jnp.int32, (n_rows, width), 1)
    own_head = (row % N_HEADS) == (col % N_HEADS)
    acc = alpha * a_scr[...]
    for p in range(npg):
        ep = jnp.dot(e[:, p * PAGE_SIZE:(p + 1) * PAGE_SIZE], spread_ref[...], preferred_element_type=F32)
        ep = jnp.where(own_head, ep, 0.0).astype(BF16)
        vp = v_refs[p][...].reshape(width, V_DIM).astype(BF16)
        acc = acc + jnp.dot(ep, vp, preferred_element_type=F32)
    a_scr[...] = acc
    m_scr[...] = jnp.broadcast_to(m_new, m_scr.shape)

    @pl.when(hf == pl.num_programs(1) - 1)
    def _():
        a = a_scr[...]
        l = l_scr[:, 0:1]
        o_ref[...] = _subln(a[:half], l[:half], a[half:], l[half:], _lam(lam_ref, li), w_ref[...], li)


def _paged(page_table, qbd, kn, vn, lam_rows, subln_w, cache_kt, cache_v, layer, li):
    batch, n_pages = page_table.shape
    npg = PAGES_PER_STEP
    steps = n_pages // npg
    n_new = kn.shape[1]
    n_rows = qbd.shape[1]

    def page_spec(shape, p):
        return pl.BlockSpec((None, None) + shape,
                            lambda b, hf, pt: (layer, pt[b * n_pages + hf * npg + p], 0, 0, 0))

    spread = jnp.repeat(jnp.eye(PAGE_SIZE, dtype=BF16), N_HEADS, axis=1)
    stat = pltpu.VMEM((n_rows, 128), F32)
    return pl.pallas_call(
        functools.partial(_paged_kernel, li=li),
        grid_spec=pltpu.PrefetchScalarGridSpec(
            num_scalar_prefetch=1,
            grid=(batch, steps),
            in_specs=[
                pl.BlockSpec((None, n_rows, D_MODEL), lambda b, hf, pt: (b, 0, 0)),
                pl.BlockSpec((None, n_new, D_MODEL), lambda b, hf, pt: (b, 0, 0)),
                pl.BlockSpec((None, n_new, N_HEADS, V_DIM), lambda b, hf, pt: (b, 0, 0, 0)),
                pl.BlockSpec((4, HEAD_DIM), lambda b, hf, pt: (0, 0)),
                pl.BlockSpec((1, V_DIM), lambda b, hf, pt: (0, 0)),
                pl.BlockSpec((PAGE_SIZE, PAGE_SIZE * N_HEADS), lambda b, hf, pt: (0, 0)),
            ]
            + [page_spec((N_QK, HEAD_DIM, PAGE_SIZE), p) for p in range(npg)]
            + [page_spec((PAGE_SIZE, N_HEADS, V_DIM), p) for p in range(npg)],
            out_specs=pl.BlockSpec((None, n_rows // 2, V_DIM), lambda b, hf, pt: (b, 0, 0)),
            scratch_shapes=[stat, stat, stat],
        ),
        out_shape=jax.ShapeDtypeStruct((batch, n_rows // 2, V_DIM), F32),
        compiler_params=_params(("arbitrary", "arbitrary")),
        name="paged",
    )(page_table.reshape(-1), qbd, kn, vn, lam_rows, subln_w.reshape(1, V_DIM), spread,
      *([cache_kt] * npg), *([cache_v] * npg))


def _log_sigmoid(x):
    return -(jnp.maximum(-x, 0.0) + jnp.log1p(jnp.exp(-jnp.abs(x))))


def _gelu_tanh(x):
    return 0.5 * x * (1.0 + jnp.tanh(math.sqrt(2.0 / math.pi) * (x + 0.044715 * (x * x * x))))


def _lru_coeffs(xc, wa_ref, ba, wx_ref, bx, lru):
    ra, ri = [], []
    for n in range(N_RNN_BLOCKS):
        xb = xc[:, n * RNN_BW:(n + 1) * RNN_BW].astype(BF16)
        ra.append(jnp.dot(xb, wa_ref[n], preferred_element_type=F32))
        ri.append(jnp.dot(xb, wx_ref[n], preferred_element_type=F32))
    r = jax.nn.sigmoid(jnp.concatenate(ra, axis=1) + ba)
    i = jax.nn.sigmoid(jnp.concatenate(ri, axis=1) + bx)
    log_a = LRU_C * r * _log_sigmoid(lru)
    a = jnp.exp(log_a)
    b = jnp.sqrt(-jnp.tanh(log_a) * (a * a + 1.0)) * (i * xc)
    return a, b


def _rnn_kernel(rx_ref, rg_ref, cw_ref, cb_ref, wa_ref, ba_ref, wx_ref, bx_ref, lru_ref,
                hg_ref, hl_ref, xext, a_scr, b_scr, h_scr, carry):
    t = pl.program_id(1)
    tc = rx_ref.shape[0]

    @pl.when(t == 0)
    def _():
        xext[0:8, :] = jnp.zeros((8, D_RNN), F32)
        carry[...] = jnp.zeros_like(carry)

    x = rx_ref[...]
    xext[8:8 + tc, :] = x
    xc = cb_ref[...] + cw_ref[3:4, :] * x
    for j in range(CONV_W - 1):
        xc = xc + cw_ref[j:j + 1, :] * xext[5 + j:5 + j + tc, :]
    xext[0:8, :] = x[tc - 8:tc, :]

    a, b = _lru_coeffs(xc, wa_ref, ba_ref[...], wx_ref, bx_ref[...], lru_ref[...])
    sub = lax.broadcasted_iota(jnp.int32, a.shape, 0) % 8
    for d in (1, 2, 4):
        keep = sub >= d
        a_sh = jnp.where(keep, pltpu.roll(a, d, 0), 1.0)
        b_sh = jnp.where(keep, pltpu.roll(b, d, 0), 0.0)
        b = a * b_sh + b
        a = a * a_sh
    a_scr[...] = a
    b_scr[...] = b

    def group(g, c):
        r = pl.multiple_of(g * 8, 8)
        h = a_scr[pl.ds(r, 8), :] * c + b_scr[pl.ds(r, 8), :]
        h_scr[pl.ds(r, 8), :] = h
        return jnp.broadcast_to(h[7:8, :], (8, D_RNN))

    c = lax.fori_loop(0, tc // 8, group, carry[...])
    carry[...] = c
    hg_ref[...] = (h_scr[...] * _gelu_tanh(rg_ref[...])).astype(BF16)

    @pl.when(t == pl.num_programs(1) - 1)
    def _():
        hl_ref[...] = c[0:1, :]


def _rnn_weight_specs():
    def full(shape):
        return pl.BlockSpec(shape, lambda *_: (0,) * len(shape))

    return [full((CONV_W, D_RNN)), full((1, D_RNN)),
            full((N_RNN_BLOCKS, RNN_BW, RNN_BW)), full((1, D_RNN)),
            full((N_RNN_BLOCKS, RNN_BW, RNN_BW)), full((1, D_RNN)), full((1, D_RNN))]


def _rnn(rx, rg, rnn_w, batch, seq):
    tc = RNN_TILE
    nt = seq // tc
    row = pl.BlockSpec((tc, D_RNN), lambda b, t: (b * nt + t, 0))
    big = pltpu.VMEM((tc, D_RNN), F32)
    return pl.pallas_call(
        _rnn_kernel,
        grid=(batch, nt),
        in_specs=[row, row] + _rnn_weight_specs(),
        out_specs=[row, pl.BlockSpec((None, 1, D_RNN), lambda b, t: (b, 0, 0))],
        out_shape=[jax.ShapeDtypeStruct((batch * seq, D_RNN), BF16),
                   jax.ShapeDtypeStruct((batch, 1, D_RNN), F32)],
        scratch_shapes=[pltpu.VMEM((tc + 8, D_RNN), F32), big, big, big, pltpu.VMEM((8, D_RNN), F32)],
        compiler_params=_params(("arbitrary", "arbitrary")),
        name="rnn",
    )(rx, rg, *rnn_w)


def _rnn_step_kernel(rx_ref, rg_ref, cs_ref, h0_ref, cw_ref, cb_ref, wa_ref, ba_ref, wx_ref, bx_ref, lru_ref,
                     hg_ref, hl_ref):
    steps = rx_ref.shape[0]
    xs = [cs_ref[j] for j in range(CONV_W - 1)] + [rx_ref[t] for t in range(steps)]
    h = h0_ref[...]
    for t in range(steps):
        xc = cb_ref[...] + sum(cw_ref[j:j + 1, :] * xs[t + j] for j in range(CONV_W))
        a, b = _lru_coeffs(xc, wa_ref, ba_ref[...], wx_ref, bx_ref[...], lru_ref[...])
        h = a * h + b
        hg_ref[t] = (h * _gelu_tanh(rg_ref[t])).astype(BF16)
    hl_ref[...] = h


def _rnn_step(rx_t, rg_t, cs_t, h0, rnn_w):
    steps, batch, _ = rx_t.shape

    def full(shape):
        return pl.BlockSpec(shape, lambda i: (0,) * len(shape))

    return pl.pallas_call(
        _rnn_step_kernel,
        grid=(1,),
        in_specs=[full(rx_t.shape), full(rg_t.shape), full(cs_t.shape), full(h0.shape)] + _rnn_weight_specs(),
        out_specs=[full(rx_t.shape), full(h0.shape)],
        out_shape=[jax.ShapeDtypeStruct(rx_t.shape, BF16), jax.ShapeDtypeStruct(h0.shape, F32)],
        compiler_params=_params(("arbitrary",)),
        name="rnn_step",
    )(rx_t, rg_t, cs_t, h0, *rnn_w)


def _merge_kernel(x_ref, o_ref, hg_ref, ga_ref, gr_ref, gt_ref, watt_ref, wrnn_ref, wo_ref, y_ref):
    ya = jnp.dot(o_ref[...].astype(BF16), watt_ref[...], preferred_element_type=F32)
    yr = jnp.dot(hg_ref[...], wrnn_ref[...], preferred_element_type=F32)
    merged = jax.nn.sigmoid(ga_ref[...]) * ya + jax.nn.sigmoid(gr_ref[...]) * yr
    y_ref[...] = x_ref[...] + gt_ref[...] * jnp.dot(merged.astype(BF16), wo_ref[...],
                                                     preferred_element_type=F32)


def _merge(x, o, hg, ga, gr, ada, w_att, w_rnn, w_o, tiles_per_group):
    n = x.shape[0]
    tm = min(ROW_TILE, n)
    row = pl.BlockSpec((tm, D_MODEL), lambda i: (i, 0))

    def full(shape):
        return pl.BlockSpec(shape, lambda i: (0, 0))

    return pl.pallas_call(
        _merge_kernel,
        grid=(n // tm,),
        in_specs=[row, row, pl.BlockSpec((tm, D_RNN), lambda i: (i, 0)), row, row,
                  _mod_spec(ada, 5, tiles_per_group),
                  full((D_MODEL, D_MODEL)), full((D_RNN, D_MODEL)), full((D_MODEL, D_MODEL))],
        out_specs=row,
        out_shape=jax.ShapeDtypeStruct((n, D_MODEL), F32),
        compiler_params=_params(("arbitrary",)),
        name="merge",
    )(x, o, hg, ga, gr, ada, w_att, w_rnn, w_o)


def _rope_tables(pos):
    half = HEAD_DIM // 2
    inv = 1.0 / (ROPE_THETA ** (jnp.arange(half, dtype=F32) * 2.0 / HEAD_DIM))
    ang = pos.astype(F32)[:, None] * inv[None, :]
    cos, sin = jnp.cos(ang), jnp.sin(ang)
    return jnp.tile(jnp.concatenate([cos, cos], axis=1), (1, 2)), jnp.tile(jnp.concatenate([-sin, sin], axis=1), (1, 2))


def kernel(x_prompt, x_sample, cache_k, cache_v, state_conv, state_h, page_table, c_prompt, c_sample, norm_w, w_ada, b_ada, ffn1_wi, ffn1_wo, ffn2_wi, ffn2_wo, w_in, lam_q1, lam_k1, lam_q2, lam_k2, subln_w, conv_w, conv_b, w_rg_a, b_rg_a, w_rg_x, b_rg_x, lru_param, w_att_br, w_rnn_br, w_o, norm_f):
    bp, tp, _ = x_prompt.shape
    bs, ts, _ = x_sample.shape
    past_len = page_table.shape[1] * PAGE_SIZE
    assert tp % ATT_TILE == 0 and tp % ROW_TILE == 0 and tp % RNN_TILE == 0
    assert page_table.shape[1] % PAGES_PER_STEP == 0 and ts == 4

    c_all = jnp.concatenate([c_prompt, jnp.repeat(c_sample, ts, axis=0)], axis=0)
    ada = _ada(c_all, w_ada, b_ada)
    ada_p = ada[:, :bp, None, :]
    ada_s = ada[:, None, bp:]
    tiles_per_seq = tp // ROW_TILE

    cos_p, sin_p = _rope_tables(jnp.arange(tp))
    cos_s, sin_s = _rope_tables(jnp.tile(past_len + jnp.arange(ts), bs))

    cache_kt = jnp.transpose(cache_k, (0, 1, 3, 4, 2))

    xp = x_prompt.reshape(bp * tp, D_MODEL)
    xs = x_sample.reshape(bs * ts, D_MODEL)
    outs_p = [[] for _ in range(4)]
    outs_s = [[] for _ in range(4)]
    s_q, s_k, s_v = D_MODEL, 2 * D_MODEL, 3 * D_MODEL
    s_rx, s_rg, s_ga = s_v + D_RNN, s_v + 2 * D_RNN, s_v + 2 * D_RNN + D_MODEL

    for l in range(DEPTH):
        li = _lambda_init(l)
        bf = lambda w: w.astype(BF16)
        wi1, wo1, wi2, wo2 = bf(ffn1_wi[l]), bf(ffn1_wo[l]), bf(ffn2_wi[l]), bf(ffn2_wo[l])
        wl = w_in[l]
        wa = bf(jnp.stack([wl[:, :s_q], wl[:, s_q:s_k], wl[:, s_k:s_v], wl[:, s_rg:s_ga], wl[:, s_ga:]]))
        wb = bf(jnp.stack([wl[:, s_v:s_rx], wl[:, s_rx:s_rg]]))
        w_att, w_rnn, w_out = bf(w_att_br[l]), bf(w_rnn_br[l]), bf(w_o[l])
        rnn_w = (conv_w[l], conv_b[l][None], bf(w_rg_a[l]), b_rg_a[l][None], bf(w_rg_x[l]), b_rg_x[l][None],
                 lru_param[l][None])
        lam_rows = jnp.stack([lam_q1[l], lam_k1[l], lam_q2[l], lam_k2[l]])
        last = l == DEPTH - 1

        xp = _ffn(xp, ada_p[l], 0, norm_w[l, 0], wi1, wo1, tiles_per_seq)
        q, k, v, ga, gr, rx, rg = _inproj(xp, ada_p[l], norm_w[l, 1], cos_p, sin_p, wa, wb, tiles_per_seq)
        o = _flash(q, k, v, lam_rows, subln_w[l], bp, tp, li)
        hg, h_last = _rnn(rx, rg, rnn_w, bp, tp)
        xp = _merge(xp, o, hg, ga, gr, ada_p[l], w_att, w_rnn, w_out, tiles_per_seq)
        xp = _ffn(xp, ada_p[l], 2, norm_w[l, 2], wi2, wo2, tiles_per_seq, norm_f if last else None)
        outs_p[0].append(k.reshape(bp, tp, N_QK, HEAD_DIM))
        outs_p[1].append(v.reshape(bp, tp, N_HEADS, V_DIM))
        outs_p[2].append(rx.reshape(bp, tp, D_RNN)[:, tp - (CONV_W - 1):])
        outs_p[3].append(h_last.reshape(bp, D_RNN))

        xs = _ffn(xs, ada_s[l], 0, norm_w[l, 0], wi1, wo1, 1)
        q, k, v, ga, gr, rx, rg = _inproj(xs, ada_s[l], norm_w[l, 1], cos_s, sin_s, wa, wb, 1)
        qe = q.reshape(bs, ts, N_HEADS, 2, HEAD_DIM)
        qbd = jnp.einsum('bqhjd,hH,jJ->bjqhHJd', qe, jnp.eye(N_HEADS, dtype=BF16), jnp.eye(2, dtype=BF16))
        qbd = qbd.reshape(bs, 2 * ts * N_HEADS, D_MODEL)
        o = _paged(page_table, qbd, k.reshape(bs, ts, D_MODEL), v.reshape(bs, ts, N_HEADS, V_DIM),
                   lam_rows, subln_w[l], cache_kt, cache_v, l, li)
        o = o.reshape(bs * ts, D_MODEL)
        tm = lambda a: a.reshape(bs, ts, D_RNN).transpose(1, 0, 2)
        cs_t = jnp.concatenate([state_conv[l].transpose(1, 0, 2), tm(rx)], axis=0)
        hg_t, h_last = _rnn_step(tm(rx), tm(rg), cs_t[:CONV_W - 1], state_h[l], rnn_w)
        hg = hg_t.transpose(1, 0, 2).reshape(bs * ts, D_RNN)
        xs = _merge(xs, o, hg, ga, gr, ada_s[l], w_att, w_rnn, w_out, 1)
        xs = _ffn(xs, ada_s[l], 2, norm_w[l, 2], wi2, wo2, 1, norm_f if last else None)
        outs_s[0].append(k.reshape(bs, ts, N_QK, HEAD_DIM))
        outs_s[1].append(v.reshape(bs, ts, N_HEADS, V_DIM))
        outs_s[2].append(cs_t[ts:].transpose(1, 0, 2))
        outs_s[3].append(h_last)

    stack = lambda xs_: jnp.stack(xs_)
    return (xp.reshape(bp, tp, D_MODEL), xs.reshape(bs, ts, D_MODEL),
            stack(outs_p[0]), stack(outs_p[1]), stack(outs_p[2]), stack(outs_p[3]),
            stack(outs_s[0]), stack(outs_s[1]), stack(outs_s[2]), stack(outs_s[3]))
```

```python
import functools
import math

import jax
import jax.numpy as jnp
from jax import lax
from jax.experimental import pallas as pl
from jax.experimental.pallas import tpu as pltpu

F32 = jnp.float32
BF16 = jnp.bfloat16

D_MODEL = 1024
DEPTH = 4
N_HEADS = 8
HEAD_DIM = 64
N_QK = 2 * N_HEADS
V_DIM = 2 * HEAD_DIM
D_RNN = 1280
N_RNN_BLOCKS = 10
RNN_BW = D_RNN // N_RNN_BLOCKS
CONV_W = 4
LRU_C = 8.0
D_FF = 2816
ROPE_THETA = 10000.0
PAGE_SIZE = 128
NORM_EPS = 1e-6
SUBLN_EPS = 1e-5
NEG_INF = -1e30

V7X_VMEM_BYTES = 64 * 1024 * 1024
VMEM_LIMIT = V7X_VMEM_BYTES - 8 * 1024 * 1024

ROW_TILE = 512
FF_TILE = 1408
ATT_TILE = 512
RNN_TILE = 256
PAGES_PER_STEP = 16


def _lambda_init(l):
    return 0.8 - 0.6 * math.exp(-0.3 * l)


def _params(sem, vmem=VMEM_LIMIT):
    return pltpu.CompilerParams(dimension_semantics=sem, vmem_limit_bytes=vmem)


def _rms(x, eps):
    return x * lax.rsqrt(jnp.mean(x * x, axis=-1, keepdims=True) + eps)


def _modulated_norm(x, nw, sc, sh):
    return (_rms(x, NORM_EPS) * nw) * (1.0 + sc) + sh


def _lam(lam_ref, li):
    a = jnp.sum(lam_ref[0:1, :] * lam_ref[1:2, :], axis=-1, keepdims=True)
    b = jnp.sum(lam_ref[2:3, :] * lam_ref[3:4, :], axis=-1, keepdims=True)
    return jnp.exp(a) - jnp.exp(b) + li


def _ada_kernel(c_ref, w_ref, b_ref, o_ref):
    c = c_ref[...]
    s = (c * jax.nn.sigmoid(c)).astype(BF16)
    o_ref[...] = jnp.dot(s, w_ref[...].astype(BF16), preferred_element_type=F32) + b_ref[...]


def _ada(c_all, w_ada, b_ada):
    m = c_all.shape[0]
    n_col = w_ada.shape[2] // D_MODEL
    return pl.pallas_call(
        _ada_kernel,
        grid=(DEPTH, n_col),
        in_specs=[
            pl.BlockSpec((m, D_MODEL), lambda l, j: (0, 0)),
            pl.BlockSpec((None, D_MODEL, D_MODEL), lambda l, j: (l, 0, j)),
            pl.BlockSpec((None, 1, D_MODEL), lambda l, j: (l, 0, j)),
        ],
        out_specs=pl.BlockSpec((None, m, D_MODEL), lambda l, j: (l, 0, j)),
        out_shape=jax.ShapeDtypeStruct((DEPTH, m, w_ada.shape[2]), F32),
        compiler_params=_params(("arbitrary", "arbitrary")),
        name="ada",
    )(c_all, w_ada, b_ada.reshape(DEPTH, 1, -1))


def _mod_spec(ada, col, tiles_per_group):
    _, r, _ = ada.shape
    return pl.BlockSpec((None, r, D_MODEL), lambda i, *_: (i // tiles_per_group, 0, col))


def _ffn_kernel(x_ref, sh_ref, sc_ref, gt_ref, nw_ref, wg_ref, wu_ref, wo_ref, *rest, final):
    if final:
        nf_ref, o_ref, h_scr, acc_scr = rest
    else:
        o_ref, h_scr, acc_scr = rest
    j = pl.program_id(1)

    @pl.when(j == 0)
    def _():
        h = _modulated_norm(x_ref[...], nw_ref[...], sc_ref[...], sh_ref[...])
        h_scr[...] = h.astype(BF16)
        acc_scr[...] = jnp.zeros_like(acc_scr)

    h = h_scr[...]
    g = jnp.dot(h, wg_ref[...], preferred_element_type=F32)
    u = jnp.dot(h, wu_ref[...], preferred_element_type=F32)
    a = (g * jax.nn.sigmoid(g) * u).astype(BF16)
    acc_scr[...] += jnp.dot(a, wo_ref[...], preferred_element_type=F32)

    @pl.when(j == pl.num_programs(1) - 1)
    def _():
        y = x_ref[...] + (0.5 * gt_ref[...]) * acc_scr[...]
        if final:
            y = _rms(y, NORM_EPS) * nf_ref[...]
        o_ref[...] = y


def _ffn(x, ada, sub, nw, wi, wo, tiles_per_group, norm_f=None):
    n = x.shape[0]
    tm = min(ROW_TILE, n)
    n_f = D_FF // FF_TILE
    final = norm_f is not None
    row = pl.BlockSpec((tm, D_MODEL), lambda i, j: (i, 0))
    vec = pl.BlockSpec((1, D_MODEL), lambda i, j: (0, 0))
    in_specs = [
        row,
        _mod_spec(ada, 3 * sub + 0, tiles_per_group),
        _mod_spec(ada, 3 * sub + 1, tiles_per_group),
        _mod_spec(ada, 3 * sub + 2, tiles_per_group),
        vec,
        pl.BlockSpec((D_MODEL, FF_TILE), lambda i, j: (0, j)),
        pl.BlockSpec((D_MODEL, FF_TILE), lambda i, j: (0, j + n_f)),
        pl.BlockSpec((FF_TILE, D_MODEL), lambda i, j: (j, 0)),
    ]
    args = [x, ada, ada, ada, nw.reshape(1, D_MODEL), wi, wi, wo]
    if final:
        in_specs.append(vec)
        args.append(norm_f.reshape(1, D_MODEL))
    return pl.pallas_call(
        functools.partial(_ffn_kernel, final=final),
        grid=(n // tm, n_f),
        in_specs=in_specs,
        out_specs=row,
        out_shape=jax.ShapeDtypeStruct((n, D_MODEL), F32),
        scratch_shapes=[pltpu.VMEM((tm, D_MODEL), BF16), pltpu.VMEM((tm, D_MODEL), F32)],
        compiler_params=_params(("arbitrary", "arbitrary")),
        name="ffn",
    )(*args)


def _rope_store(o_ref, r, cos, sin, scale):
    lane = lax.broadcasted_iota(jnp.int32, cos.shape, 1)
    first = (lane % HEAD_DIM) < (HEAD_DIM // 2)
    for c in range(D_MODEL // 128):
        blk = r[:, c * 128:(c + 1) * 128]
        swapped = jnp.where(first, pltpu.roll(blk, 128 - HEAD_DIM // 2, 1),
                            pltpu.roll(blk, HEAD_DIM // 2, 1))
        out = blk * cos + swapped * sin
        if scale != 1.0:
            out = out * scale
        o_ref[:, c * 128:(c + 1) * 128] = out.astype(o_ref.dtype)


def _inproj_kernel(x_ref, sh_ref, sc_ref, nw_ref, cos_ref, sin_ref, *refs, rope_scales):
    n_out = len(rope_scales)
    w_refs, o_refs = refs[:n_out], refs[n_out:]
    h = _modulated_norm(x_ref[...], nw_ref[...], sc_ref[...], sh_ref[...]).astype(BF16)
    for w_ref, o_ref, scale in zip(w_refs, o_refs, rope_scales):
        y = jnp.dot(h, w_ref[...], preferred_element_type=F32)
        if scale is None:
            o_ref[...] = y.astype(o_ref.dtype)
        else:
            _rope_store(o_ref, y, cos_ref[...], sin_ref[...], scale)


def _inproj(x, ada, nw, cos, sin, weights, out_dtypes, rope_scales, tiles_per_group):
    n = x.shape[0]
    tm = min(ROW_TILE, n)
    n_pos_tiles = cos.shape[0] // tm
    tab = pl.BlockSpec((tm, 128), lambda i: (i % n_pos_tiles, 0))

    def row(width):
        return pl.BlockSpec((tm, width), lambda i: (i, 0))

    return pl.pallas_call(
        functools.partial(_inproj_kernel, rope_scales=tuple(rope_scales)),
        grid=(n // tm,),
        in_specs=[
            row(D_MODEL),
            _mod_spec(ada, 3, tiles_per_group),
            _mod_spec(ada, 4, tiles_per_group),
            pl.BlockSpec((1, D_MODEL), lambda i: (0, 0)),
            tab, tab,
        ] + [pl.BlockSpec(w.shape, lambda i: (0, 0)) for w in weights],
        out_specs=[row(w.shape[1]) for w in weights],
        out_shape=[jax.ShapeDtypeStruct((n, w.shape[1]), dt) for w, dt in zip(weights, out_dtypes)],
        compiler_params=_params(("arbitrary",)),
        name="inproj",
    )(x, ada, ada, nw.reshape(1, D_MODEL), cos, sin, *weights)


def _subln(o1, l1, o2, l2, lam, w, li):
    o = o1 / l1 - lam * (o2 / l2)
    return (_rms(o, SUBLN_EPS) * w) * (1.0 - li)


def _flash_kernel(q_ref, k_ref, v_ref, lam_ref, w_ref, o_ref, kt_scr, vb_scr, *, li):
    seq = q_ref.shape[0]
    tq = ATT_TILE
    kt_scr[...] = k_ref[...].T.astype(BF16)
    vb_scr[...] = v_ref[...].astype(BF16)
    lam = _lam(lam_ref, li)
    w = w_ref[...]
    lane = lax.broadcasted_iota(jnp.int32, (tq, 2 * HEAD_DIM), 1)
    causal = (lax.broadcasted_iota(jnp.int32, (tq, tq), 1) <= lax.broadcasted_iota(jnp.int32, (tq, tq), 0))
    for i in range(seq // tq):
        lo = i * tq
        q = q_ref[lo:lo + tq, :]
        outs = []
        for sel in (lane < HEAD_DIM, lane >= HEAD_DIM):
            qm = jnp.where(sel, q, jnp.zeros_like(q))
            sd = jnp.dot(qm, kt_scr[:, lo:lo + tq], preferred_element_type=F32)
            sd = jnp.where(causal, sd, NEG_INF)
            m = jnp.max(sd, axis=-1, keepdims=True)
            if i:
                so = jnp.dot(qm, kt_scr[:, :lo], preferred_element_type=F32)
                m = jnp.maximum(m, jnp.max(so, axis=-1, keepdims=True))
            ed = jnp.exp(sd - m)
            l = jnp.sum(ed, axis=-1, keepdims=True)
            acc = jnp.dot(ed.astype(BF16), vb_scr[lo:lo + tq, :], preferred_element_type=F32)
            if i:
                eo = jnp.exp(so - m)
                l = l + jnp.sum(eo, axis=-1, keepdims=True)
                acc = acc + jnp.dot(eo.astype(BF16), vb_scr[:lo, :], preferred_element_type=F32)
            outs.append(acc / l)
        o_ref[lo:lo + tq, :] = (_rms(outs[0] - lam * outs[1], SUBLN_EPS) * w) * (1.0 - li)


def _flash(q, k, v, lam_rows, subln_w, batch, seq, li):
    spec = pl.BlockSpec((seq, 2 * HEAD_DIM), lambda b, h: (b, h))
    return pl.pallas_call(
        functools.partial(_flash_kernel, li=li),
        grid=(batch, N_HEADS),
        in_specs=[spec, spec, spec,
                  pl.BlockSpec((4, HEAD_DIM), lambda b, h: (0, 0)),
                  pl.BlockSpec((1, V_DIM), lambda b, h: (0, 0))],
        out_specs=spec,
        out_shape=jax.ShapeDtypeStruct((batch * seq, D_MODEL), F32),
        scratch_shapes=[pltpu.VMEM((2 * HEAD_DIM, seq), BF16), pltpu.VMEM((seq, 2 * HEAD_DIM), BF16)],
        compiler_params=_params(("arbitrary", "arbitrary")),
        name="flash",
    )(q, k, v, lam_rows, subln_w.reshape(1, V_DIM))


def _paged_kernel(pt_ref, qbd_ref, kn_ref, vn_ref, lam_ref, w_ref, spread_ref, *rest, li):
    npg = PAGES_PER_STEP
    k_refs, v_refs = rest[:npg], rest[npg:2 * npg]
    o_ref, m_scr, l_scr, a_scr = rest[2 * npg:]
    hf = pl.program_id(1)
    qbd = qbd_ref[...]
    n_new = kn_ref.shape[0]
    n_rows = qbd.shape[0]
    half = n_rows // 2
    width = PAGE_SIZE * N_HEADS

    @pl.when(hf == 0)
    def _():
        nt = (((1,), (1,)), ((), ()))
        s = lax.dot_general(qbd, kn_ref[...].astype(BF16), nt, preferred_element_type=F32)
        row = lax.broadcasted_iota(jnp.int32, s.shape, 0)
        col = lax.broadcasted_iota(jnp.int32, s.shape, 1)
        s = jnp.where(col <= (row % half) // N_HEADS, s, NEG_INF)
        m = jnp.max(s, axis=-1, keepdims=True)
        e = jnp.exp(s - m)
        acc = jnp.zeros(a_scr.shape, F32)
        for t in range(n_new):
            acc = acc + e[:, t:t + 1] * jnp.tile(vn_ref[t], (n_rows // N_HEADS, 1))
        m_scr[...] = jnp.broadcast_to(m, m_scr.shape)
        l_scr[...] = jnp.broadcast_to(jnp.sum(e, axis=-1, keepdims=True), l_scr.shape)
        a_scr[...] = acc

    kt = jnp.concatenate([k_refs[p][...].reshape(N_QK * HEAD_DIM, PAGE_SIZE).astype(BF16) for p in range(npg)],
                         axis=1)
    s = jnp.dot(qbd, kt, preferred_element_type=F32)
    m_old = m_scr[:, 0:1]
    m_new = jnp.maximum(m_old, jnp.max(s, axis=-1, keepdims=True))
    alpha = jnp.exp(m_old - m_new)
    e = jnp.exp(s - m_new)
    l_scr[...] = jnp.broadcast_to(alpha * l_scr[:, 0:1] + jnp.sum(e, axis=-1, keepdims=True), l_scr.shape)
    e = e.astype(BF16)
    row = lax.broadcasted_iota(jnp.int32, (n_rows, width), 0)
    col = lax.broadcasted_iota(jnp.int32, (n_rows, width), 1)
    own_head = (row % N_HEADS) == (col % N_HEADS)
    acc = alpha * a_scr[...]
    for p in range(npg):
        ep = jnp.dot(e[:, p * PAGE_SIZE:(p + 1) * PAGE_SIZE], spread_ref[...], preferred_element_type=F32)
        ep = jnp.where(own_head, ep, 0.0).astype(BF16)
        vp = v_refs[p][...].reshape(width, V_DIM).astype(BF16)
        acc = acc + jnp.dot(ep, vp, preferred_element_type=F32)
    a_scr[...] = acc
    m_scr[...] = jnp.broadcast_to(m_new, m_scr.shape)

    @pl.when(hf == pl.num_programs(1) - 1)
    def _():
        a = a_scr[...]
        l = l_scr[:, 0:1]
        o_ref[...] = _subln(a[:half], l[:half], a[half:], l[half:], _lam(lam_ref, li), w_ref[...], li)


def _paged(page_table, qbd, kn, vn, lam_rows, subln_w, cache_kt, cache_v, layer, li):
    batch, n_pages = page_table.shape
    npg = PAGES_PER_STEP
    steps = n_pages // npg
    n_new = kn.shape[1]
    n_rows = qbd.shape[1]

    def page_spec(shape, p):
        return pl.BlockSpec((None, None) + shape,
                            lambda b, hf, pt: (layer, pt[b * n_pages + hf * npg + p], 0, 0, 0))

    spread = jnp.repeat(jnp.eye(PAGE_SIZE, dtype=BF16), N_HEADS, axis=1)
    stat = pltpu.VMEM((n_rows, 128), F32)
    return pl.pallas_call(
        functools.partial(_paged_kernel, li=li),
        grid_spec=pltpu.PrefetchScalarGridSpec(
            num_scalar_prefetch=1,
            grid=(batch, steps),
            in_specs=[
                pl.BlockSpec((None, n_rows, D_MODEL), lambda b, hf, pt: (b, 0, 0)),
                pl.BlockSpec((None, n_new, D_MODEL), lambda b, hf, pt: (b, 0, 0)),
                pl.BlockSpec((None, n_new, N_HEADS, V_DIM), lambda b, hf, pt: (b, 0, 0, 0)),
                pl.BlockSpec((4, HEAD_DIM), lambda b, hf, pt: (0, 0)),
                pl.BlockSpec((1, V_DIM), lambda b, hf, pt: (0, 0)),
                pl.BlockSpec((PAGE_SIZE, PAGE_SIZE * N_HEADS), lambda b, hf, pt: (0, 0)),
            ]
            + [page_spec((N_QK, HEAD_DIM, PAGE_SIZE), p) for p in range(npg)]
            + [page_spec((PAGE_SIZE, N_HEADS, V_DIM), p) for p in range(npg)],
            out_specs=pl.BlockSpec((None, n_rows // 2, V_DIM), lambda b, hf, pt: (b, 0, 0)),
            scratch_shapes=[stat, stat, stat],
        ),
        out_shape=jax.ShapeDtypeStruct((batch, n_rows // 2, V_DIM), F32),
        compiler_params=_params(("arbitrary", "arbitrary")),
        name="paged",
    )(page_table.reshape(-1), qbd, kn, vn, lam_rows, subln_w.reshape(1, V_DIM), spread,
      *([cache_kt] * npg), *([cache_v] * npg))


def _log_sigmoid(x):
    return -(jnp.maximum(-x, 0.0) + jnp.log1p(jnp.exp(-jnp.abs(x))))


def _gelu_tanh(x):
    return 0.5 * x * (1.0 + jnp.tanh(math.sqrt(2.0 / math.pi) * (x + 0.044715 * (x * x * x))))


def _lru_coeffs(xc, wa_ref, ba, wx_ref, bx, lru):
    ra, ri = [], []
    for n in range(N_RNN_BLOCKS):
        xb = xc[:, n * RNN_BW:(n + 1) * RNN_BW].astype(BF16)
        ra.append(jnp.dot(xb, wa_ref[n], preferred_element_type=F32))
        ri.append(jnp.dot(xb, wx_ref[n], preferred_element_type=F32))
    r = jax.nn.sigmoid(jnp.concatenate(ra, axis=1) + ba)
    i = jax.nn.sigmoid(jnp.concatenate(ri, axis=1) + bx)
    log_a = LRU_C * r * _log_sigmoid(lru)
    a = jnp.exp(log_a)
    b = jnp.sqrt(-jnp.tanh(log_a) * (a * a + 1.0)) * (i * xc)
    return a, b


def _rnn_kernel(rx_ref, rg_ref, cw_ref, cb_ref, wa_ref, ba_ref, wx_ref, bx_ref, lru_ref,
                hg_ref, hl_ref, xext, a_scr, b_scr, h_scr, carry):
    t = pl.program_id(1)
    tc = rx_ref.shape[0]

    @pl.when(t == 0)
    def _():
        xext[0:8, :] = jnp.zeros((8, D_RNN), F32)
        carry[...] = jnp.zeros_like(carry)

    x = rx_ref[...]
    xext[8:8 + tc, :] = x
    xc = cb_ref[...] + cw_ref[3:4, :] * x
    for j in range(CONV_W - 1):
        xc = xc + cw_ref[j:j + 1, :] * xext[5 + j:5 + j + tc, :]
    xext[0:8, :] = x[tc - 8:tc, :]

    a, b = _lru_coeffs(xc, wa_ref, ba_ref[...], wx_ref, bx_ref[...], lru_ref[...])
    sub = lax.broadcasted_iota(jnp.int32, a.shape, 0) % 8
    for d in (1, 2, 4):
        keep = sub >= d
        a_sh = jnp.where(keep, pltpu.roll(a, d, 0), 1.0)
        b_sh = jnp.where(keep, pltpu.roll(b, d, 0), 0.0)
        b = a * b_sh + b
        a = a * a_sh
    a_scr[...] = a
    b_scr[...] = b

    def group(g, c):
        r = pl.multiple_of(g * 8, 8)
        h = a_scr[pl.ds(r, 8), :] * c + b_scr[pl.ds(r, 8), :]
        h_scr[pl.ds(r, 8), :] = h
        return jnp.broadcast_to(h[7:8, :], (8, D_RNN))

    c = lax.fori_loop(0, tc // 8, group, carry[...])
    carry[...] = c
    hg_ref[...] = (h_scr[...] * _gelu_tanh(rg_ref[...])).astype(BF16)

    @pl.when(t == pl.num_programs(1) - 1)
    def _():
        hl_ref[...] = c[0:1, :]


def _rnn_weight_specs():
    def full(shape):
        return pl.BlockSpec(shape, lambda *_: (0,) * len(shape))

    return [full((CONV_W, D_RNN)), full((1, D_RNN)),
            full((N_RNN_BLOCKS, RNN_BW, RNN_BW)), full((1, D_RNN)),
            full((N_RNN_BLOCKS, RNN_BW, RNN_BW)), full((1, D_RNN)), full((1, D_RNN))]


def _rnn(rx, rg, rnn_w, batch, seq):
    tc = RNN_TILE
    nt = seq // tc
    row = pl.BlockSpec((tc, D_RNN), lambda b, t: (b * nt + t, 0))
    big = pltpu.VMEM((tc, D_RNN), F32)
    return pl.pallas_call(
        _rnn_kernel,
        grid=(batch, nt),
        in_specs=[row, row] + _rnn_weight_specs(),
        out_specs=[row, pl.BlockSpec((None, 1, D_RNN), lambda b, t: (b, 0, 0))],
        out_shape=[jax.ShapeDtypeStruct((batch * seq, D_RNN), BF16),
                   jax.ShapeDtypeStruct((batch, 1, D_RNN), F32)],
        scratch_shapes=[pltpu.VMEM((tc + 8, D_RNN), F32), big, big, big, pltpu.VMEM((8, D_RNN), F32)],
        compiler_params=_params(("arbitrary", "arbitrary")),
        name="rnn",
    )(rx, rg, *rnn_w)


def _rnn_step_kernel(rx_ref, rg_ref, cs_ref, h0_ref, cw_ref, cb_ref, wa_ref, ba_ref, wx_ref, bx_ref, lru_ref,
                     hg_ref, hl_ref):
    steps = rx_ref.shape[0]
    xs = [cs_ref[j] for j in range(CONV_W - 1)] + [rx_ref[t] for t in range(steps)]
    h = h0_ref[...]
    for t in range(steps):
        xc = cb_ref[...] + sum(cw_ref[j:j + 1, :] * xs[t + j] for j in range(CONV_W))
        a, b = _lru_coeffs(xc, wa_ref, ba_ref[...], wx_ref, bx_ref[...], lru_ref[...])
        h = a * h + b
        hg_ref[t] = (h * _gelu_tanh(rg_ref[t])).astype(BF16)
    hl_ref[...] = h


def _rnn_step(rx_t, rg_t, cs_t, h0, rnn_w):
    steps, batch, _ = rx_t.shape

    def full(shape):
        return pl.BlockSpec(shape, lambda i: (0,) * len(shape))

    return pl.pallas_call(
        _rnn_step_kernel,
        grid=(1,),
        in_specs=[full(rx_t.shape), full(rg_t.shape), full(cs_t.shape), full(h0.shape)] + _rnn_weight_specs(),
        out_specs=[full(rx_t.shape), full(h0.shape)],
        out_shape=[jax.ShapeDtypeStruct(rx_t.shape, BF16), jax.ShapeDtypeStruct(h0.shape, F32)],
        compiler_params=_params(("arbitrary",)),
        name="rnn_step",
    )(rx_t, rg_t, cs_t, h0, *rnn_w)


def _merge_kernel(x_ref, o_ref, hg_ref, ga_ref, gr_ref, gt_ref, watt_ref, wrnn_ref, wo_ref, y_ref):
    ya = jnp.dot(o_ref[...].astype(BF16), watt_ref[...], preferred_element_type=F32)
    yr = jnp.dot(hg_ref[...], wrnn_ref[...], preferred_element_type=F32)
    merged = jax.nn.sigmoid(ga_ref[...]) * ya + jax.nn.sigmoid(gr_ref[...]) * yr
    y_ref[...] = x_ref[...] + gt_ref[...] * jnp.dot(merged.astype(BF16), wo_ref[...],
                                                     preferred_element_type=F32)


def _merge(x, o, hg, ga, gr, ada, w_att, w_rnn, w_o, tiles_per_group):
    n = x.shape[0]
    tm = min(ROW_TILE, n)
    row = pl.BlockSpec((tm, D_MODEL), lambda i: (i, 0))

    def full(shape):
        return pl.BlockSpec(shape, lambda i: (0, 0))

    return pl.pallas_call(
        _merge_kernel,
        grid=(n // tm,),
        in_specs=[row, row, pl.BlockSpec((tm, D_RNN), lambda i: (i, 0)), row, row,
                  _mod_spec(ada, 5, tiles_per_group),
                  full((D_MODEL, D_MODEL)), full((D_RNN, D_MODEL)), full((D_MODEL, D_MODEL))],
        out_specs=row,
        out_shape=jax.ShapeDtypeStruct((n, D_MODEL), F32),
        compiler_params=_params(("arbitrary",)),
        name="merge",
    )(x, o, hg, ga, gr, ada, w_att, w_rnn, w_o)


def _rope_tables(pos):
    half = HEAD_DIM // 2
    inv = 1.0 / (ROPE_THETA ** (jnp.arange(half, dtype=F32) * 2.0 / HEAD_DIM))
    ang = pos.astype(F32)[:, None] * inv[None, :]
    cos, sin = jnp.cos(ang), jnp.sin(ang)
    return jnp.tile(jnp.concatenate([cos, cos], axis=1), (1, 2)), jnp.tile(jnp.concatenate([-sin, sin], axis=1), (1, 2))


def kernel(x_prompt, x_sample, cache_k, cache_v, state_conv, state_h, page_table, c_prompt, c_sample, norm_w, w_ada, b_ada, ffn1_wi, ffn1_wo, ffn2_wi, ffn2_wo, w_in, lam_q1, lam_k1, lam_q2, lam_k2, subln_w, conv_w, conv_b, w_rg_a, b_rg_a, w_rg_x, b_rg_x, lru_param, w_att_br, w_rnn_br, w_o, norm_f):
    bp, tp, _ = x_prompt.shape
    bs, ts, _ = x_sample.shape
    past_len = page_table.shape[1] * PAGE_SIZE
    assert tp % ATT_TILE == 0 and tp % ROW_TILE == 0 and tp % RNN_TILE == 0
    assert page_table.shape[1] % PAGES_PER_STEP == 0 and ts == 4

    c_all = jnp.concatenate([c_prompt, jnp.repeat(c_sample, ts, axis=0)], axis=0)
    ada = _ada(c_all, w_ada, b_ada)
    ada_p = ada[:, :bp, None, :]
    ada_s = ada[:, None, bp:]
    tiles_per_seq = tp // ROW_TILE

    cos_p, sin_p = _rope_tables(jnp.arange(tp))
    cos_s, sin_s = _rope_tables(jnp.tile(past_len + jnp.arange(ts), bs))

    cache_kt = jnp.transpose(cache_k, (0, 1, 3, 4, 2))

    xp = x_prompt.reshape(bp * tp, D_MODEL)
    xs = x_sample.reshape(bs * ts, D_MODEL)
    outs_p = [[] for _ in range(4)]
    outs_s = [[] for _ in range(4)]
    s_q, s_k, s_v = D_MODEL, 2 * D_MODEL, 3 * D_MODEL
    s_rx, s_rg, s_ga = s_v + D_RNN, s_v + 2 * D_RNN, s_v + 2 * D_RNN + D_MODEL

    for l in range(DEPTH):
        li = _lambda_init(l)
        bf = lambda w: w.astype(BF16)
        wi1, wo1, wi2, wo2 = bf(ffn1_wi[l]), bf(ffn1_wo[l]), bf(ffn2_wi[l]), bf(ffn2_wo[l])
        wl = w_in[l]
        w_qkv = [bf(wl[:, :s_q]), bf(wl[:, s_q:s_k]), bf(wl[:, s_k:s_v])]
        w_rest = [bf(wl[:, s_rg:s_ga]), bf(wl[:, s_ga:]), bf(wl[:, s_v:s_rx]), bf(wl[:, s_rx:s_rg])]

        def inproj(x, ada_l, cos, sin, tiles):
            q, k, v = _inproj(x, ada_l, norm_w[l, 1], cos, sin, w_qkv, (BF16, F32, F32),
                              (HEAD_DIM ** -0.5, 1.0, None), tiles)
            ga, gr, rx, rg = _inproj(x, ada_l, norm_w[l, 1], cos, sin, w_rest, (F32,) * 4, (None,) * 4, tiles)
            return q, k, v, ga, gr, rx, rg
        w_att, w_rnn, w_out = bf(w_att_br[l]), bf(w_rnn_br[l]), bf(w_o[l])
        rnn_w = (conv_w[l], conv_b[l][None], bf(w_rg_a[l]), b_rg_a[l][None], bf(w_rg_x[l]), b_rg_x[l][None],
                 lru_param[l][None])
        lam_rows = jnp.stack([lam_q1[l], lam_k1[l], lam_q2[l], lam_k2[l]])
        last = l == DEPTH - 1

        xp = _ffn(xp, ada_p[l], 0, norm_w[l, 0], wi1, wo1, tiles_per_seq)
        q, k, v, ga, gr, rx, rg = inproj(xp, ada_p[l], cos_p, sin_p, tiles_per_seq)
        o = _flash(q, k, v, lam_rows, subln_w[l], bp, tp, li)
        hg, h_last = _rnn(rx, rg, rnn_w, bp, tp)
        xp = _merge(xp, o, hg, ga, gr, ada_p[l], w_att, w_rnn, w_out, tiles_per_seq)
        xp = _ffn(xp, ada_p[l], 2, norm_w[l, 2], wi2, wo2, tiles_per_seq, norm_f if last else None)
        outs_p[0].append(k.reshape(bp, tp, N_QK, HEAD_DIM))
        outs_p[1].append(v.reshape(bp, tp, N_HEADS, V_DIM))
        outs_p[2].append(rx.reshape(bp, tp, D_RNN)[:, tp - (CONV_W - 1):])
        outs_p[3].append(h_last.reshape(bp, D_RNN))

        xs = _ffn(xs, ada_s[l], 0, norm_w[l, 0], wi1, wo1, 1)
        q, k, v, ga, gr, rx, rg = inproj(xs, ada_s[l], cos_s, sin_s, 1)
        qe = q.reshape(bs, ts, N_HEADS, 2, HEAD_DIM)
        qbd = jnp.einsum('bqhjd,hH,jJ->bjqhHJd', qe, jnp.eye(N_HEADS, dtype=BF16), jnp.eye(2, dtype=BF16))
        qbd = qbd.reshape(bs, 2 * ts * N_HEADS, D_MODEL)
        o = _paged(page_table, qbd, k.reshape(bs, ts, D_MODEL), v.reshape(bs, ts, N_HEADS, V_DIM),
                   lam_rows, subln_w[l], cache_kt, cache_v, l, li)
        o = o.reshape(bs * ts, D_MODEL)
        tm = lambda a: a.reshape(bs, ts, D_RNN).transpose(1, 0, 2)
        cs_t = jnp.concatenate([state_conv[l].transpose(1, 0, 2), tm(rx)], axis=0)
        hg_t, h_last = _rnn_step(tm(rx), tm(rg), cs_t[:CONV_W - 1], state_h[l], rnn_w)
        hg = hg_t.transpose(1, 0, 2).reshape(bs * ts, D_RNN)
        xs = _merge(xs, o, hg, ga, gr, ada_s[l], w_att, w_rnn, w_out, 1)
        xs = _ffn(xs, ada_s[l], 2, norm_w[l, 2], wi2, wo2, 1, norm_f if last else None)
        outs_s[0].append(k.reshape(bs, ts, N_QK, HEAD_DIM))
        outs_s[1].append(v.reshape(bs, ts, N_HEADS, V_DIM))
        outs_s[2].append(cs_t[ts:].transpose(1, 0, 2))
        outs_s[3].append(h_last)

    stack = lambda xs_: jnp.stack(xs_)
    return (xp.reshape(bp, tp, D_MODEL), xs.reshape(bs, ts, D_MODEL),
            stack(outs_p[0]), stack(outs_p[1]), stack(outs_p[2]), stack(outs_p[3]),
            stack(outs_s[0]), stack(outs_s[1]), stack(outs_s[2]), stack(outs_s[3]))
```
